```python
import math
import jax
import jax.numpy as jnp
from jax import lax
import numpy as np

D_MODEL = 2048
BATCH = 4
SEQ = 2048
DEPTH = 4
DEC_BATCH = 128
DEC_SEQ = 4
PAST_LEN = 8192
PAGE_SIZE = 128

N_MIXERS = 3
N_MLA = (DEPTH + 2) // 3
N_S5 = (DEPTH + 1) // 3
N_RWKV = DEPTH // 3
EPS = 1e-6

MLA_HEADS = 16
QK_NOPE = 128
QK_ROPE = 64
V_HEAD = 128
KV_LORA = 512
LATENT = KV_LORA + QK_ROPE
QK_HEAD = QK_NOPE + QK_ROPE
Q_DIM = MLA_HEADS * QK_HEAD
MLA_WIDTH = MLA_HEADS * V_HEAD
MLA_IN = Q_DIM + LATENT + MLA_WIDTH
ROPE_THETA = 10000.0
SM_SCALE = QK_HEAD ** -0.5
Q_BLOCK = 128
NEG_INIT = -1e30

S5_WIDTH = D_MODEL
S5_GROUP = 16
S5_GROUPS = S5_WIDTH // S5_GROUP
S5_STATE = 64
S5_DT_MIN = 1e-3
S5_DT_MAX = 1e-1

RWKV_WIDTH = D_MODEL
RWKV_HEAD = 64
RWKV_HEADS = RWKV_WIDTH // RWKV_HEAD
DECAY_LORA = 96
AAA_LORA = 96
GN_EPS = 64e-5

kernel_name = 'hybrid_mla_s5_rwkv7_step'


def _rms(x, g):
    xf = x.astype(jnp.float32)
    y = xf * lax.rsqrt(jnp.mean(xf * xf, axis=-1, keepdims=True) + EPS)
    return (y * g.astype(jnp.float32)).astype(x.dtype)


def _rope_tables(pos):
    inv = ROPE_THETA ** (-jnp.arange(0, QK_ROPE, 2, dtype=jnp.float32) / QK_ROPE)
    ang = pos.astype(jnp.float32)[:, None] * inv[None, :]
    return jnp.cos(ang), jnp.sin(ang)


def _rope(x, cos, sin):
    xf = x.astype(jnp.float32)
    x1, x2 = xf[..., :QK_ROPE // 2], xf[..., QK_ROPE // 2:]
    return jnp.concatenate([x1 * cos - x2 * sin, x1 * sin + x2 * cos], axis=-1).astype(x.dtype)


def _mla_project(h, pos, w_in, g_qn, g_qr, g_kv, g_kr):
    nb, t, _ = h.shape
    z = h @ w_in
    q = z[..., :Q_DIM].reshape(nb, t, MLA_HEADS, QK_HEAD)
    c = z[..., Q_DIM:Q_DIM + KV_LORA]
    kr = z[..., Q_DIM + KV_LORA:Q_DIM + LATENT]
    gate = z[..., Q_DIM + LATENT:]
    cos, sin = _rope_tables(pos)
    qn = _rms(q[..., :QK_NOPE], g_qn)
    qr = _rope(_rms(q[..., QK_NOPE:], g_qr), cos[:, None, :], sin[:, None, :])
    lat = jnp.concatenate([_rms(c, g_kv), _rope(_rms(kr, g_kr), cos, sin)], axis=-1)
    return qn, qr, lat, gate


def _k_nope(c, w_uk, g_kn):
    return _rms(jnp.einsum('bkc,chd->bkhd', c, w_uk), g_kn)


def _scores(qn, qr, kn, kr):
    s = jnp.einsum('bqhd,bkhd->bhqk', qn, kn) + jnp.einsum('bqhd,bkd->bhqk', qr, kr)
    return s.astype(jnp.float32) * SM_SCALE


def _mla_prompt_attn(qn, qr, lat, w_uk, w_uv, g_kn):
    nb, s_len = qn.shape[0], qn.shape[1]
    c, kr = lat[..., :KV_LORA], lat[..., KV_LORA:]
    kn = _k_nope(c, w_uk, g_kn)
    v = jnp.einsum('bkc,chd->bkhd', c, w_uv)
    n_blk = s_len // Q_BLOCK

    def to_blocks(t):
        return jnp.moveaxis(t.reshape((nb, n_blk, Q_BLOCK) + t.shape[2:]), 1, 0)

    kpos = jnp.arange(s_len)

    def block(args):
        b_idx, qn_b, qr_b = args
        qpos = b_idx * Q_BLOCK + jnp.arange(Q_BLOCK)
        s = _scores(qn_b, qr_b, kn, kr)
        s = jnp.where(kpos[None, :] <= qpos[:, None], s, -jnp.inf)
        p = jax.nn.softmax(s, axis=-1).astype(v.dtype)
        return jnp.einsum('bhqk,bkhd->bqhd', p, v)

    o = lax.map(block, (jnp.arange(n_blk), to_blocks(qn), to_blocks(qr)))
    return jnp.moveaxis(o, 0, 1).reshape(nb, s_len, MLA_WIDTH)


def _mla_sample_attn(qn, qr, lat_new, cache, page_table, w_uk, w_uv, g_kn):
    nb, t = qn.shape[0], qn.shape[1]

    def update(carry, lat, mask):
        m, l, acc = carry
        c, kr = lat[..., :KV_LORA], lat[..., KV_LORA:]
        s = _scores(qn, qr, _k_nope(c, w_uk, g_kn), kr)
        if mask is not None:
            s = jnp.where(mask, s, -jnp.inf)
        m_new = jnp.maximum(m, jnp.max(s, axis=-1))
        corr = jnp.exp(m - m_new)
        p = jnp.exp(s - m_new[..., None])
        l = l * corr + jnp.sum(p, axis=-1)
        acc = acc * corr[..., None] + jnp.einsum('bhqk,bkc->bhqc', p, c.astype(jnp.float32))
        return (m_new, l, acc)

    def page_step(carry, phys):
        return update(carry, cache[phys], None), None

    init = (jnp.full((nb, MLA_HEADS, t), NEG_INIT, jnp.float32),
            jnp.zeros((nb, MLA_HEADS, t), jnp.float32),
            jnp.zeros((nb, MLA_HEADS, t, KV_LORA), jnp.float32))
    carry, _ = lax.scan(page_step, init, page_table.T)
    causal = jnp.tril(jnp.ones((t, t), dtype=bool))
    m, l, acc = update(carry, lat_new, causal)
    o_lat = (acc / l[..., None]).astype(w_uv.dtype)
    o = jnp.einsum('bhqc,chd->bqhd', o_lat, w_uv)
    return o.reshape(nb, t, MLA_WIDTH)


def _s5_combine(e1, e2):
    a1r, a1i, b1r, b1i = e1
    a2r, a2i, b2r, b2i = e2
    return (a2r * a1r - a2i * a1i, a2r * a1i + a2i * a1r,
            a2r * b1r - a2i * b1i + b2r, a2r * b1i + a2i * b1r + b2i)


def _s5_discretize(lam_re, lam_im, log_dt, b_re, b_im):
    dt = jnp.exp(log_dt.astype(jnp.float32))[:, None]
    lr, li = lam_re.astype(jnp.float32), lam_im.astype(jnp.float32)
    mag = jnp.exp(lr * dt)
    ab_re, ab_im = mag * jnp.cos(li * dt), mag * jnp.sin(li * dt)
    nr, ni = ab_re - 1.0, ab_im
    den = lr * lr + li * li
    fr, fi = (nr * lr + ni * li) / den, (ni * lr - nr * li) / den
    br, bi = b_re.astype(jnp.float32), b_im.astype(jnp.float32)
    bb_re = fr[..., None] * br - fi[..., None] * bi
    bb_im = fr[..., None] * bi + fi[..., None] * br
    return ab_re, ab_im, bb_re, bb_im


def _s5_mix(u, x0_re, x0_im, lam_re, lam_im, log_dt, b_re, b_im, c_re, c_im, d_skip):
    nb, t, _ = u.shape
    uf = u.astype(jnp.float32)
    ug = uf.reshape(nb, t, S5_GROUPS, S5_GROUP)
    ab_re, ab_im, bb_re, bb_im = _s5_discretize(lam_re, lam_im, log_dt, b_re, b_im)
    bu_re = jnp.einsum('btgj,gnj->btgn', ug, bb_re)
    bu_im = jnp.einsum('btgj,gnj->btgn', ug, bb_im)
    a_re = jnp.broadcast_to(ab_re, (1, t) + ab_re.shape)
    a_im = jnp.broadcast_to(ab_im, (1, t) + ab_im.shape)
    p_re, p_im, s_re, s_im = lax.associative_scan(_s5_combine, (a_re, a_im, bu_re, bu_im), axis=1)
    x0r = x0_re.astype(jnp.float32)[:, None]
    x0i = x0_im.astype(jnp.float32)[:, None]
    xr = s_re + p_re * x0r - p_im * x0i
    xi = s_im + p_re * x0i + p_im * x0r
    y = (jnp.einsum('gjn,btgn->btgj', c_re.astype(jnp.float32), xr)
         - jnp.einsum('gjn,btgn->btgj', c_im.astype(jnp.float32), xi))
    y = y.reshape(nb, t, S5_WIDTH) + d_skip.astype(jnp.float32) * uf
    return y.astype(u.dtype), xr[:, -1], xi[:, -1]


def _s5_branch(h, x0_re, x0_im, w_in, lam_re, lam_im, log_dt, b_re, b_im, c_re, c_im,
               d_skip, w_glu, b_glu, w_o):
    z = h @ w_in
    u, gate = z[..., :S5_WIDTH], z[..., S5_WIDTH:]
    y, xr, xi = _s5_mix(u, x0_re, x0_im, lam_re, lam_im, log_dt, b_re, b_im, c_re, c_im, d_skip)
    g = jax.nn.gelu(y)
    g = g * jax.nn.sigmoid(g @ w_glu + b_glu)
    return (g * jax.nn.silu(gate)) @ w_o, xr, xi


def _rwkv7_step(S, inp):
    r, w, k, v, kk, b = inp
    sa = jnp.einsum('bhvk,bhk->bhv', S, -kk)
    S = S * w[:, :, None, :] + sa[..., None] * b[:, :, None, :] + v[..., None] * k[:, :, None, :]
    return S, jnp.einsum('bhvk,bhk->bhv', S, r)


def _rwkv_branch(h, shift0, s0, mu, w_r, w_k, w_v, w_g, w_o, w0, w1, w2, a0, a1, a2,
                 k_k, k_a, r_k, gn_w, gn_b):
    nb, t, _ = h.shape
    f32 = jnp.float32
    prev = jnp.concatenate([shift0[:, None, :].astype(h.dtype), h[:, :-1]], axis=1)
    xx = prev - h
    xr, xw, xk, xv, xa, xg = [h + xx * mu[n] for n in range(6)]
    r = xr @ w_r
    k = xk @ w_k
    v = xv @ w_v
    wl = -jax.nn.softplus(-(w0 + jnp.tanh(xw @ w1) @ w2).astype(f32)) - 0.5
    decay = jnp.exp(-jnp.exp(wl))
    a = jax.nn.sigmoid((a0 + (xa @ a1) @ a2).astype(f32))
    hd = lambda z: z.reshape(nb, t, RWKV_HEADS, RWKV_HEAD)
    kk = hd((k * k_k).astype(f32))
    kk = kk / jnp.maximum(jnp.sqrt(jnp.sum(kk * kk, axis=-1, keepdims=True)), 1e-12)
    kf = k.astype(f32) * (1.0 + (a - 1.0) * k_a.astype(f32))
    rh, wh, kh, vh, ah = hd(r.astype(f32)), hd(decay), hd(kf), hd(v.astype(f32)), hd(a)
    seq = tuple(jnp.moveaxis(z, 1, 0) for z in (rh, wh, kh, vh, kk, kk * ah))
    s_fin, ys = lax.scan(_rwkv7_step, s0.astype(f32), seq)
    y = jnp.moveaxis(ys, 0, 1)
    mean = jnp.mean(y, axis=-1, keepdims=True)
    var = jnp.mean(jnp.square(y - mean), axis=-1, keepdims=True)
    y = ((y - mean) * lax.rsqrt(var + GN_EPS)).reshape(nb, t, RWKV_WIDTH) * gn_w + gn_b
    bonus = jnp.sum(rh * kh * r_k.astype(f32), axis=-1, keepdims=True) * vh
    y = y + bonus.reshape(nb, t, RWKV_WIDTH)
    gate = jax.nn.silu(xg @ w_g)
    return (y.astype(h.dtype) * gate) @ w_o, h[:, -1], s_fin


def setup_inputs(seed: int = 0) -> dict:
    key = jax.random.key(seed)
    ks = iter(jax.random.split(key, 64))
    f32 = jnp.float32

    def nrm(shape, scale=1.0):
        return scale * jax.random.normal(next(ks), shape, f32)

    def gain(shape):
        return 1.0 + 0.02 * jax.random.normal(next(ks), shape, f32)

    def unif(shape, lo, hi):
        return jax.random.uniform(next(ks), shape, f32, lo, hi)

    n_pages = PAST_LEN // PAGE_SIZE
    n_used = DEC_BATCH * n_pages
    n_phys = n_used + n_used // 4
    page_table = jax.random.permutation(next(ks), n_phys)[:n_used].reshape(DEC_BATCH, n_pages).astype(jnp.int32)
    d = D_MODEL
    lam_im = math.pi * jnp.arange(S5_STATE, dtype=f32)[None, None, :] + nrm((N_S5, S5_GROUPS, S5_STATE), 0.05)
    return {
        'x_prompt': nrm((BATCH, SEQ, d)),
        'x_sample': nrm((DEC_BATCH, DEC_SEQ, d)),
        'cache_mla': nrm((N_MLA, n_phys, PAGE_SIZE, LATENT)),
        'page_table': page_table,
        'state_s5_re': nrm((N_S5, DEC_BATCH, S5_GROUPS, S5_STATE), 0.1),
        'state_s5_im': nrm((N_S5, DEC_BATCH, S5_GROUPS, S5_STATE), 0.1),
        'state_rwkv': nrm((N_RWKV, DEC_BATCH, RWKV_HEADS, RWKV_HEAD, RWKV_HEAD), 0.1),
        'state_rwkv_shift': nrm((N_RWKV, DEC_BATCH, d)),
        'norm_g': gain((DEPTH, d)),
        'mla_w_in': nrm((N_MLA, d, MLA_IN), d ** -0.5),
        'mla_g_qn': gain((N_MLA, QK_NOPE)),
        'mla_g_qr': gain((N_MLA, QK_ROPE)),
        'mla_g_kv': gain((N_MLA, KV_LORA)),
        'mla_g_kr': gain((N_MLA, QK_ROPE)),
        'mla_g_kn': gain((N_MLA, QK_NOPE)),
        'mla_w_uk': nrm((N_MLA, KV_LORA, MLA_HEADS, QK_NOPE), KV_LORA ** -0.5),
        'mla_w_uv': nrm((N_MLA, KV_LORA, MLA_HEADS, V_HEAD), KV_LORA ** -0.5),
        'mla_w_o': nrm((N_MLA, MLA_WIDTH, d), MLA_WIDTH ** -0.5),
        's5_w_in': nrm((N_S5, d, 2 * S5_WIDTH), d ** -0.5),
        's5_lam_re': -0.5 * jnp.exp(nrm((N_S5, S5_GROUPS, S5_STATE), 0.1)),
        's5_lam_im': lam_im,
        's5_log_dt': unif((N_S5, S5_GROUPS), math.log(S5_DT_MIN), math.log(S5_DT_MAX)),
        's5_b_re': nrm((N_S5, S5_GROUPS, S5_STATE, S5_GROUP), (2 * S5_GROUP) ** -0.5),
        's5_b_im': nrm((N_S5, S5_GROUPS, S5_STATE, S5_GROUP), (2 * S5_GROUP) ** -0.5),
        's5_c_re': nrm((N_S5, S5_GROUPS, S5_GROUP, S5_STATE), (2 * S5_STATE) ** -0.5),
        's5_c_im': nrm((N_S5, S5_GROUPS, S5_GROUP, S5_STATE), (2 * S5_STATE) ** -0.5),
        's5_d': nrm((N_S5, S5_WIDTH)),
        's5_w_glu': nrm((N_S5, S5_WIDTH, S5_WIDTH), S5_WIDTH ** -0.5),
        's5_b_glu': nrm((N_S5, S5_WIDTH), 0.01),
        's5_w_o': nrm((N_S5, S5_WIDTH, d), S5_WIDTH ** -0.5),
        'rwkv_mu': unif((N_RWKV, 6, d), 0.0, 1.0),
        'rwkv_w_r': nrm((N_RWKV, d, RWKV_WIDTH), d ** -0.5),
        'rwkv_w_k': nrm((N_RWKV, d, RWKV_WIDTH), d ** -0.5),
        'rwkv_w_v': nrm((N_RWKV, d, RWKV_WIDTH), d ** -0.5),
        'rwkv_w_g': nrm((N_RWKV, d, RWKV_WIDTH), d ** -0.5),
        'rwkv_w_o': nrm((N_RWKV, RWKV_WIDTH, d), RWKV_WIDTH ** -0.5),
        'rwkv_w0': unif((N_RWKV, RWKV_WIDTH), -6.0, 1.0),
        'rwkv_w1': nrm((N_RWKV, d, DECAY_LORA), d ** -0.5),
        'rwkv_w2': nrm((N_RWKV, DECAY_LORA, RWKV_WIDTH), 0.1 * DECAY_LORA ** -0.5),
        'rwkv_a0': nrm((N_RWKV, RWKV_WIDTH), 0.1),
        'rwkv_a1': nrm((N_RWKV, d, AAA_LORA), d ** -0.5),
        'rwkv_a2': nrm((N_RWKV, AAA_LORA, RWKV_WIDTH), 0.1 * AAA_LORA ** -0.5),
        'rwkv_k_k': 0.85 + nrm((N_RWKV, RWKV_WIDTH), 0.05),
        'rwkv_k_a': 1.0 + nrm((N_RWKV, RWKV_WIDTH), 0.05),
        'rwkv_r_k': nrm((N_RWKV, RWKV_HEADS, RWKV_HEAD), 0.1),
        'rwkv_gn_w': gain((N_RWKV, RWKV_WIDTH)),
        'rwkv_gn_b': nrm((N_RWKV, RWKV_WIDTH), 0.01),
    }


def reference(x_prompt, x_sample, cache_mla, page_table, state_s5_re, state_s5_im, state_rwkv,
              state_rwkv_shift, norm_g,
              mla_w_in, mla_g_qn, mla_g_qr, mla_g_kv, mla_g_kr, mla_g_kn, mla_w_uk, mla_w_uv, mla_w_o,
              s5_w_in, s5_lam_re, s5_lam_im, s5_log_dt, s5_b_re, s5_b_im, s5_c_re, s5_c_im, s5_d,
              s5_w_glu, s5_b_glu, s5_w_o,
              rwkv_mu, rwkv_w_r, rwkv_w_k, rwkv_w_v, rwkv_w_g, rwkv_w_o, rwkv_w0, rwkv_w1, rwkv_w2,
              rwkv_a0, rwkv_a1, rwkv_a2, rwkv_k_k, rwkv_k_a, rwkv_r_k, rwkv_gn_w, rwkv_gn_b):
    n_prompt, seq = x_prompt.shape[0], x_prompt.shape[1]
    t_dec = x_sample.shape[1]
    past = page_table.shape[1] * PAGE_SIZE
    pos_p = jnp.arange(seq, dtype=jnp.int32)
    pos_s = past + jnp.arange(t_dec, dtype=jnp.int32)
    yp, ys = x_prompt, x_sample
    mla_p, mla_s = [], []
    s5r_p, s5i_p, s5r_s, s5i_s = [], [], [], []
    rw_p, rw_s, sh_p, sh_s = [], [], [], []
    for i in range(DEPTH):
        kind, j = i % N_MIXERS, i // N_MIXERS
        hp, hs = _rms(yp, norm_g[i]), _rms(ys, norm_g[i])
        if kind == 0:
            proj = (mla_w_in[j], mla_g_qn[j], mla_g_qr[j], mla_g_kv[j], mla_g_kr[j])
            qn, qr, lat, gate = _mla_project(hp, pos_p, *proj)
            o = _mla_prompt_attn(qn, qr, lat, mla_w_uk[j], mla_w_uv[j], mla_g_kn[j])
            yp = yp + (o * jax.nn.silu(gate)) @ mla_w_o[j]
            mla_p.append(lat)
            qn, qr, lat, gate = _mla_project(hs, pos_s, *proj)
            o = _mla_sample_attn(qn, qr, lat, cache_mla[j], page_table, mla_w_uk[j], mla_w_uv[j], mla_g_kn[j])
            ys = ys + (o * jax.nn.silu(gate)) @ mla_w_o[j]
            mla_s.append(lat)
        elif kind == 1:
            prm = (s5_w_in[j], s5_lam_re[j], s5_lam_im[j], s5_log_dt[j], s5_b_re[j], s5_b_im[j],
                   s5_c_re[j], s5_c_im[j], s5_d[j], s5_w_glu[j], s5_b_glu[j], s5_w_o[j])
            zero = jnp.zeros((n_prompt, S5_GROUPS, S5_STATE), jnp.float32)
            out, xr, xi = _s5_branch(hp, zero, zero, *prm)
            yp = yp + out
            s5r_p.append(xr)
            s5i_p.append(xi)
            out, xr, xi = _s5_branch(hs, state_s5_re[j], state_s5_im[j], *prm)
            ys = ys + out
            s5r_s.append(xr)
            s5i_s.append(xi)
        else:
            prm = (rwkv_mu[j], rwkv_w_r[j], rwkv_w_k[j], rwkv_w_v[j], rwkv_w_g[j], rwkv_w_o[j],
                   rwkv_w0[j], rwkv_w1[j], rwkv_w2[j], rwkv_a0[j], rwkv_a1[j], rwkv_a2[j],
                   rwkv_k_k[j], rwkv_k_a[j], rwkv_r_k[j], rwkv_gn_w[j], rwkv_gn_b[j])
            sh0 = jnp.zeros((n_prompt, D_MODEL), yp.dtype)
            st0 = jnp.zeros((n_prompt, RWKV_HEADS, RWKV_HEAD, RWKV_HEAD), jnp.float32)
            out, sh, st = _rwkv_branch(hp, sh0, st0, *prm)
            yp = yp + out
            sh_p.append(sh)
            rw_p.append(st)
            out, sh, st = _rwkv_branch(hs, state_rwkv_shift[j], state_rwkv[j], *prm)
            ys = ys + out
            sh_s.append(sh)
            rw_s.append(st)
    return (yp, ys, jnp.stack(mla_p), jnp.stack(mla_s),
            jnp.stack(s5r_p), jnp.stack(s5i_p), jnp.stack(s5r_s), jnp.stack(s5i_s),
            jnp.stack(rw_p), jnp.stack(rw_s), jnp.stack(sh_p), jnp.stack(sh_s))
```

```python
import functools
import math

import jax
import jax.numpy as jnp
from jax import lax
from jax.experimental import pallas as pl
from jax.experimental.pallas import tpu as pltpu

F32 = jnp.float32
BF16 = jnp.bfloat16

EPS = 1e-6
PAGE = 128
HEADS = 16
D_NOPE = 128
D_ROPE = 64
D_QK = D_NOPE + D_ROPE
KV_LORA = 512
LATENT = KV_LORA + D_ROPE
ROPE_THETA = 10000.0
SM_SCALE = D_QK ** -0.5
NEG = -1e30
S5_GROUP = 16
S5_STATE = 64
S5_GB = 16
RW_HEAD = 64
RW_CHUNK = 64
GN_EPS = 64e-5
LANE = 128
VMEM_LIMIT = 48 * 1024 * 1024

_NT = (((1,), (1,)), ((), ()))
_TN = (((0,), (0,)), ((), ()))


def _cparams(sem):
    return pltpu.CompilerParams(dimension_semantics=sem, vmem_limit_bytes=VMEM_LIMIT)


def _sigmoid(x):
    return 1.0 / (1.0 + jnp.exp(-x))


def _silu(x):
    return x * _sigmoid(x)


def _gelu_tanh(x):
    return 0.5 * x * (1.0 + jnp.tanh(0.7978845608028654 * (x + 0.044715 * x * x * x)))


def _softplus(x):
    return jnp.maximum(x, 0.0) + jnp.log(1.0 + jnp.exp(-jnp.abs(x)))


def _mm_body(*refs, pro, n_pro, epi, n_epi, n_out):
    pro_refs = refs[:n_pro]
    w_ref = refs[n_pro]
    epi_refs = refs[n_pro + 1:n_pro + 1 + n_epi]
    out_refs = refs[n_pro + 1 + n_epi:n_pro + 1 + n_epi + n_out]
    xs_ref = refs[-1]

    @pl.when(pl.program_id(1) == 0)
    def _():
        xs_ref[...] = pro(*[r[...] for r in pro_refs]).astype(BF16)

    acc = jnp.dot(xs_ref[...], w_ref[...], preferred_element_type=F32)
    epi(acc, epi_refs, out_refs)


def _mm(name, pro, pro_args, pro_specs, w, epi, epi_args, epi_specs, out_shapes, out_specs, tm, tn):
    m = pro_args[0].shape[0]
    k, n = w.shape
    assert m % tm == 0 and n % tn == 0, (name, m, tm, n, tn)
    body = functools.partial(_mm_body, pro=pro, n_pro=len(pro_args), epi=epi, n_epi=len(epi_args),
                             n_out=len(out_shapes))
    return pl.pallas_call(
        body,
        grid=(m // tm, n // tn),
        in_specs=[*pro_specs, pl.BlockSpec((k, tn), lambda i, j: (0, j)), *epi_specs],
        out_specs=out_specs,
        out_shape=out_shapes,
        scratch_shapes=[pltpu.VMEM((tm, k), BF16)],
        compiler_params=_cparams(("parallel", "arbitrary")),
        name=name,
    )(*pro_args, w, *epi_args)


def _row_spec(tm, k):
    return pl.BlockSpec((tm, k), lambda i, j: (i, 0))


def _vec_spec(k):
    return pl.BlockSpec((1, k), lambda i, j: (0, 0))


def _tile_spec(tm, tn):
    return pl.BlockSpec((tm, tn), lambda i, j: (i, j))


def _nvec_spec(tn):
    return pl.BlockSpec((1, tn), lambda i, j: (0, j))


def _pro_rms(x, g):
    return x * lax.rsqrt(jnp.mean(x * x, axis=-1, keepdims=True) + EPS) * g


def _pro_gate(a, b):
    return a * _silu(b)


def _pro_mix(h, prev, mu):
    return h + (prev - h) * mu


def _pro_cast(x):
    return x


def _pro_tanh(x):
    return jnp.tanh(x)


def _epi_plain(acc, epi_refs, out_refs):
    out_refs[0][...] = acc.astype(out_refs[0].dtype)


def _epi_res(acc, epi_refs, out_refs):
    out_refs[0][...] = epi_refs[0][...] + acc


def _epi_glu(acc, epi_refs, out_refs):
    g_ref, b_ref = epi_refs
    out_refs[0][...] = g_ref[...] * _sigmoid(acc + b_ref[...])


def _pick_tm(m):
    return 512 if m % 512 == 0 else m


def _pick_tn(n):
    for t in (512, 256, 128):
        if n % t == 0:
            return t
    return n


def _mm_simple(name, pro, pro_args, pro_kinds, w, epi=_epi_plain, epi_args=(), epi_kinds=(),
               out_dtype=F32):
    m = pro_args[0].shape[0]
    k, n = w.shape
    tm, tn = _pick_tm(m), _pick_tn(n)
    pro_specs = [_row_spec(tm, k) if kd == "row" else _vec_spec(k) for kd in pro_kinds]
    epi_specs = [_tile_spec(tm, tn) if kd == "tile" else _nvec_spec(tn) for kd in epi_kinds]
    return _mm(name, pro, list(pro_args), pro_specs, w, epi, list(epi_args), epi_specs,
               [jax.ShapeDtypeStruct((m, n), out_dtype)], [_tile_spec(tm, tn)], tm, tn)[0]


def _rms_body(x_ref, g_ref, o_ref):
    o_ref[...] = _pro_rms(x_ref[...], g_ref[...])


def _rms_rows(x, g):
    m, k = x.shape
    tm = _pick_tm(m)
    return pl.pallas_call(
        _rms_body,
        grid=(m // tm,),
        in_specs=[pl.BlockSpec((tm, k), lambda i: (i, 0)), pl.BlockSpec((1, k), lambda i: (0, 0))],
        out_specs=pl.BlockSpec((tm, k), lambda i: (i, 0)),
        out_shape=jax.ShapeDtypeStruct((m, k), F32),
        compiler_params=_cparams(("parallel",)),
        name="rms_rows",
    )(x, g)


def _epi_qnope(acc, epi_refs, out_refs, *, four_d):
    g_ref, = epi_refs
    o_ref, = out_refs
    for hh in range(acc.shape[1] // LANE):
        a = acc[:, hh * LANE:(hh + 1) * LANE]
        y = a * lax.rsqrt(jnp.mean(a * a, axis=-1, keepdims=True) + EPS) * g_ref[...] * SM_SCALE
        if four_d:
            o_ref[0, hh] = y.astype(o_ref.dtype)
        else:
            o_ref[:, hh * LANE:(hh + 1) * LANE] = y.astype(o_ref.dtype)


def _rope_pair_tile(x, g2, c4, s4):
    lane = lax.broadcasted_iota(jnp.int32, x.shape, 1)
    lo = lane < D_ROPE
    x2 = x * x
    ss_e = jnp.sum(jnp.where(lo, x2, 0.0), axis=-1, keepdims=True)
    ss_o = jnp.sum(jnp.where(lo, 0.0, x2), axis=-1, keepdims=True)
    inv = jnp.where(lo, lax.rsqrt(ss_e * (1.0 / D_ROPE) + EPS), lax.rsqrt(ss_o * (1.0 / D_ROPE) + EPS))
    xn = x * inv * g2
    first_half = (lane % D_ROPE) < (D_ROPE // 2)
    partner = jnp.where(first_half, pltpu.roll(xn, LANE - D_ROPE // 2, 1), pltpu.roll(xn, D_ROPE // 2, 1))
    return (xn * c4 + partner * s4) * SM_SCALE, lo


def _epi_qrope(acc, epi_refs, out_refs, *, four_d):
    g2_ref, c4_ref, s4_ref = epi_refs
    o_ref, = out_refs
    for p in range(acc.shape[1] // LANE):
        r, lo = _rope_pair_tile(acc[:, p * LANE:(p + 1) * LANE], g2_ref[...], c4_ref[...], s4_ref[...])
        if four_d:
            o_ref[0, 2 * p] = jnp.where(lo, r, 0.0).astype(o_ref.dtype)
            o_ref[0, 2 * p + 1] = jnp.where(lo, 0.0, r).astype(o_ref.dtype)
        else:
            o_ref[:, p * LANE:(p + 1) * LANE] = r.astype(o_ref.dtype)


def _epi_latent(acc, epi_refs, out_refs):
    gkv_ref, g2_ref, g2p_ref, c4_ref, s4_ref = epi_refs
    lat_ref, cn_ref, kr_ref = out_refs
    c = acc[:, :KV_LORA]
    cn = c * lax.rsqrt(jnp.mean(c * c, axis=-1, keepdims=True) + EPS) * gkv_ref[...]
    lat_ref[:, :KV_LORA] = cn
    cn_ref[...] = cn.astype(cn_ref.dtype)
    d = acc[:, KV_LORA:KV_LORA + LANE]
    dp = acc[:, KV_LORA + LANE:KV_LORA + 2 * LANE]
    inv = lax.rsqrt(jnp.mean(d * d, axis=-1, keepdims=True) + EPS)
    r = d * inv * g2_ref[...] * c4_ref[...] + dp * inv * g2p_ref[...] * s4_ref[...]
    lat_ref[:, KV_LORA:LATENT] = r[:, :D_ROPE]
    kr_ref[...] = r.astype(kr_ref.dtype)


def _epi_knope(acc, epi_refs, out_refs):
    g_ref, = epi_refs
    o_ref, = out_refs
    for hh in range(acc.shape[1] // LANE):
        a = acc[:, hh * LANE:(hh + 1) * LANE]
        y = a * lax.rsqrt(jnp.mean(a * a, axis=-1, keepdims=True) + EPS) * g_ref[...]
        o_ref[0, hh] = y.astype(o_ref.dtype)


def _epi_heads(acc, epi_refs, out_refs):
    o_ref, = out_refs
    for hh in range(acc.shape[1] // LANE):
        o_ref[0, hh] = acc[:, hh * LANE:(hh + 1) * LANE].astype(o_ref.dtype)


def _flash_body(qn_ref, qr_ref, kn_ref, kr_ref, v_ref, o_ref, m_ref, l_ref, acc_ref, *, tq, tk, nk):
    qi = pl.program_id(2)
    ki = pl.program_id(3)

    @pl.when(ki == 0)
    def _():
        m_ref[...] = jnp.full(m_ref.shape, NEG, F32)
        l_ref[...] = jnp.zeros(l_ref.shape, F32)
        acc_ref[...] = jnp.zeros(acc_ref.shape, F32)

    @pl.when(ki * tk <= qi * tq + (tq - 1))
    def _():
        q = jnp.concatenate([qn_ref[0, 0], qr_ref[0, 0]], axis=-1)
        k = jnp.concatenate([kn_ref[0, 0], kr_ref[0]], axis=-1)
        s = lax.dot_general(q, k, _NT, preferred_element_type=F32)
        qpos = qi * tq + lax.broadcasted_iota(jnp.int32, s.shape, 0)
        kpos = ki * tk + lax.broadcasted_iota(jnp.int32, s.shape, 1)
        s = jnp.where(kpos <= qpos, s, NEG)
        m_old = m_ref[...]
        m_new = jnp.maximum(m_old, jnp.max(s, axis=-1, keepdims=True))
        alpha = jnp.exp(m_old - m_new)
        p = jnp.exp(s - m_new)
        l_ref[...] = alpha * l_ref[...] + jnp.sum(p, axis=-1, keepdims=True)
        acc_ref[...] = alpha * acc_ref[...] + jnp.dot(p.astype(BF16), v_ref[0, 0], preferred_element_type=F32)
        m_ref[...] = m_new

    @pl.when(ki == nk - 1)
    def _():
        o_ref[0] = acc_ref[...] / l_ref[...]


def _flash_attention(qn, qr, kn, kr, v):
    b, h, s, _ = qn.shape
    tq = tk = min(512, s)
    nq, nk = s // tq, s // tk

    def kv_idx(qi, ki):
        return jnp.minimum(ki, (qi * tq + tq - 1) // tk)

    body = functools.partial(_flash_body, tq=tq, tk=tk, nk=nk)
    return pl.pallas_call(
        body,
        grid=(b, h, nq, nk),
        in_specs=[
            pl.BlockSpec((1, 1, tq, LANE), lambda bi, hi, qi, ki: (bi, hi, qi, 0)),
            pl.BlockSpec((1, 1, tq, LANE), lambda bi, hi, qi, ki: (bi, hi, qi, 0)),
            pl.BlockSpec((1, 1, tk, LANE), lambda bi, hi, qi, ki: (bi, hi, kv_idx(qi, ki), 0)),
            pl.BlockSpec((1, tk, LANE), lambda bi, hi, qi, ki: (bi, kv_idx(qi, ki), 0)),
            pl.BlockSpec((1, 1, tk, LANE), lambda bi, hi, qi, ki: (bi, hi, kv_idx(qi, ki), 0)),
        ],
        out_specs=pl.BlockSpec((1, tq, LANE), lambda bi, hi, qi, ki: (bi, qi, hi)),
        out_shape=jax.ShapeDtypeStruct((b, s, h * LANE), F32),
        scratch_shapes=[pltpu.VMEM((tq, 1), F32), pltpu.VMEM((tq, 1), F32), pltpu.VMEM((tq, LANE), F32)],
        compiler_params=_cparams(("parallel", "parallel", "parallel", "arbitrary")),
        name="mla_prompt_attn",
    )(qn, qr, kn, kr, v)


def _dec_attn_body(pt_ref, *refs, pps, n_steps, t_dec):
    cache_refs = refs[:pps]
    wuk_ref, qp_ref, qr_ref, cnew_ref, krnew_ref, wuv_ref = refs[pps:pps + 6]
    o_ref = refs[pps + 6]
    wq_ref, m_ref, l_ref, acc_ref = refs[pps + 7:]
    del pt_ref
    i = pl.program_id(1)
    n_up = HEADS * D_NOPE
    rows = t_dec * HEADS

    @pl.when(i == 0)
    def _():
        wq_ref[0:n_up, :] = wuk_ref[...]
        wq_ref[n_up:n_up + rows, :] = qp_ref[0]
        m_ref[...] = jnp.full(m_ref.shape, NEG, F32)
        l_ref[...] = jnp.zeros(l_ref.shape, F32)
        acc_ref[...] = jnp.zeros(acc_ref.shape, F32)

    def process(c_bf, kr_bf, mask):
        n = c_bf.shape[0]
        res = lax.dot_general(wq_ref[...], c_bf, _NT, preferred_element_type=F32)
        kn = res[:n_up].reshape(HEADS, D_NOPE, n)
        inv = lax.rsqrt(jnp.sum(kn * kn, axis=1) * (1.0 / D_NOPE) + EPS)
        s = res[n_up:] * jnp.concatenate([inv] * t_dec, axis=0)
        s = s + lax.dot_general(qr_ref[0], kr_bf, _NT, preferred_element_type=F32)
        if mask is not None:
            s = jnp.where(mask, s, NEG)
        m_old = m_ref[...]
        m_new = jnp.maximum(m_old, jnp.max(s, axis=-1, keepdims=True))
        alpha = jnp.exp(m_old - m_new)
        p = jnp.exp(s - m_new)
        l_ref[...] = alpha * l_ref[...] + jnp.sum(p, axis=-1, keepdims=True)
        acc_ref[...] = alpha * acc_ref[...] + jnp.dot(p.astype(BF16), c_bf, preferred_element_type=F32)
        m_ref[...] = m_new

    for u in range(0, pps, 2):
        lat0 = cache_refs[u][0]
        lat1 = cache_refs[u + 1][0]
        c_bf = jnp.concatenate([lat0[:, :KV_LORA], lat1[:, :KV_LORA]], axis=0).astype(BF16)
        kr_bf = jnp.concatenate([lat0[:, KV_LORA:LATENT], lat1[:, KV_LORA:LATENT]], axis=0).astype(BF16)
        process(c_bf, kr_bf, None)

    @pl.when(i == n_steps - 1)
    def _():
        npad = cnew_ref.shape[1]
        key = lax.broadcasted_iota(jnp.int32, (rows, npad), 1)
        tq = lax.broadcasted_iota(jnp.int32, (rows, npad), 0) // HEADS
        process(cnew_ref[0], krnew_ref[0], key <= tq)
        o_lat = (acc_ref[...] / l_ref[...]).astype(BF16)
        full = jnp.dot(o_lat, wuv_ref[...], preferred_element_type=F32)
        rh = lax.broadcasted_iota(jnp.int32, full.shape, 0) % HEADS
        ch = lax.broadcasted_iota(jnp.int32, full.shape, 1) // LANE
        diag = jnp.where(rh == ch, full, 0.0).reshape(t_dec, HEADS, full.shape[1])
        o_ref[0] = jnp.sum(diag, axis=1)


def _dec_attention(cache, page_table, wuk_t, qp, qr, cnew, krnew, wuv):
    bdec, n_pages = page_table.shape
    rows = qp.shape[1]
    t_dec = rows // HEADS
    pps = 8 if n_pages % 8 == 0 else 2
    n_steps = n_pages // pps
    n_up = HEADS * D_NOPE

    def cache_spec(u):
        return pl.BlockSpec((1, PAGE, LATENT), lambda b, i, pt: (pt[b, i * pps + u], 0, 0))

    body = functools.partial(_dec_attn_body, pps=pps, n_steps=n_steps, t_dec=t_dec)
    grid_spec = pltpu.PrefetchScalarGridSpec(
        num_scalar_prefetch=1,
        grid=(bdec, n_steps),
        in_specs=[
            *[cache_spec(u) for u in range(pps)],
            pl.BlockSpec((n_up, KV_LORA), lambda b, i, pt: (0, 0)),
            pl.BlockSpec((1, rows, KV_LORA), lambda b, i, pt: (b, 0, 0)),
            pl.BlockSpec((1, rows, D_ROPE), lambda b, i, pt: (b, 0, 0)),
            pl.BlockSpec((1, PAGE, KV_LORA), lambda b, i, pt: (b, 0, 0)),
            pl.BlockSpec((1, PAGE, D_ROPE), lambda b, i, pt: (b, 0, 0)),
            pl.BlockSpec((KV_LORA, n_up), lambda b, i, pt: (0, 0)),
        ],
        out_specs=pl.BlockSpec((1, t_dec, n_up), lambda b, i, pt: (b, 0, 0)),
        scratch_shapes=[
            pltpu.VMEM((n_up + rows, KV_LORA), BF16),
            pltpu.VMEM((rows, 1), F32),
            pltpu.VMEM((rows, 1), F32),
            pltpu.VMEM((rows, KV_LORA), F32),
        ],
    )
    return pl.pallas_call(
        body,
        grid_spec=grid_spec,
        out_shape=jax.ShapeDtypeStruct((bdec, t_dec, n_up), F32),
        compiler_params=_cparams(("parallel", "arbitrary")),
        name="mla_sample_attn",
    )(page_table, *([cache] * pps), wuk_t, qp, qr, cnew, krnew, wuv)


def _absorb_body(qn_ref, g_ref, wuk_ref, o_ref):
    q = (qn_ref[...] * g_ref[...]).astype(BF16)
    o_ref[0] = lax.dot_general(q, wuk_ref[...], _NT, preferred_element_type=F32).astype(o_ref.dtype)


def _absorb_q(qn, g_kn, wuk):
    m = qn.shape[0]
    return pl.pallas_call(
        _absorb_body,
        grid=(HEADS,),
        in_specs=[
            pl.BlockSpec((m, LANE), lambda h: (0, h)),
            pl.BlockSpec((1, LANE), lambda h: (0, 0)),
            pl.BlockSpec((KV_LORA, LANE), lambda h: (0, h)),
        ],
        out_specs=pl.BlockSpec((1, m, KV_LORA), lambda h: (h, 0, 0)),
        out_shape=jax.ShapeDtypeStruct((HEADS, m, KV_LORA), BF16),
        compiler_params=_cparams(("parallel",)),
        name="mla_absorb_q",
    )(qn, g_kn, wuk)


def _rope_tables(pos):
    inv = ROPE_THETA ** (-jnp.arange(0, D_ROPE, 2, dtype=F32) / D_ROPE)
    ang = pos.astype(F32)[:, None] * inv[None, :]
    c, s = jnp.cos(ang), jnp.sin(ang)
    return jnp.tile(c, (1, 4)), jnp.concatenate([-s, s, -s, s], axis=1)


def _mla_weights(w_in, g_qn, g_qr, g_kv, g_kr, g_kn, w_uk, w_uv, w_o):
    d = w_in.shape[0]
    q_dim = HEADS * D_QK
    wq = w_in[:, :q_dim].reshape(d, HEADS, D_QK)
    w_kr = w_in[:, q_dim + KV_LORA:q_dim + LATENT]
    w_kr_sw = jnp.concatenate([w_kr[:, D_ROPE // 2:], w_kr[:, :D_ROPE // 2]], axis=1)
    g_kr_sw = jnp.concatenate([g_kr[D_ROPE // 2:], g_kr[:D_ROPE // 2]])
    wuk2 = w_uk.reshape(KV_LORA, HEADS * D_NOPE)
    return dict(
        w_qn=wq[:, :, :D_NOPE].reshape(d, HEADS * D_NOPE).astype(BF16),
        w_qr=wq[:, :, D_NOPE:].reshape(d, HEADS * D_ROPE).astype(BF16),
        w_lat=jnp.concatenate([w_in[:, q_dim:q_dim + KV_LORA], w_kr, w_kr, w_kr_sw, w_kr_sw], axis=1).astype(BF16),
        w_gate=w_in[:, q_dim + LATENT:].astype(BF16),
        g_qn=g_qn.reshape(1, D_NOPE),
        g_qr2=jnp.tile(g_qr, 2).reshape(1, LANE),
        g_kv=g_kv.reshape(1, KV_LORA),
        g_kr2=jnp.tile(g_kr, 2).reshape(1, LANE),
        g_kr2p=jnp.tile(g_kr_sw, 2).reshape(1, LANE),
        g_kn=g_kn.reshape(1, D_NOPE),
        wuk=wuk2.astype(BF16),
        wuk_t=wuk2.T.astype(BF16),
        wuv=w_uv.reshape(KV_LORA, HEADS * D_NOPE).astype(BF16),
        w_o=w_o.astype(BF16),
    )


def _mla_project(x2, g_norm, wts, c4, s4, seq_4d):
    m, d = x2.shape
    tm = _pick_tm(m)
    four_d = seq_4d is not None
    n_tab = c4.shape[0] // tm
    tab_spec = pl.BlockSpec((tm, LANE), lambda i, j: (i % n_tab, 0))
    lane_vec = pl.BlockSpec((1, LANE), lambda i, j: (0, 0))
    pro_args, pro_specs = [x2, g_norm], [_row_spec(tm, d), _vec_spec(d)]

    def q_out(n_heads_blk, n_cols):
        if four_d:
            nsb = seq_4d // tm
            shape = jax.ShapeDtypeStruct((m // seq_4d, HEADS, seq_4d, LANE), BF16)
            spec = pl.BlockSpec((1, n_heads_blk, tm, LANE), lambda i, j: (i // nsb, j, i % nsb, 0))
        else:
            shape = jax.ShapeDtypeStruct((m, n_cols), F32)
            spec = _tile_spec(tm, 512)
        return [shape], [spec]

    shp, spc = q_out(4, HEADS * D_NOPE)
    qn = _mm("mla_qnope", _pro_rms, pro_args, pro_specs, wts["w_qn"],
             functools.partial(_epi_qnope, four_d=four_d), [wts["g_qn"]], [lane_vec], shp, spc, tm, 512)[0]
    shp, spc = q_out(8, HEADS * D_ROPE)
    qr = _mm("mla_qrope", _pro_rms, pro_args, pro_specs, wts["w_qr"],
             functools.partial(_epi_qrope, four_d=four_d), [wts["g_qr2"], c4, s4],
             [lane_vec, tab_spec, tab_spec], shp, spc, tm, 512)[0]
    n_lat = wts["w_lat"].shape[1]
    lat, cn, kr = _mm(
        "mla_latent", _pro_rms, pro_args, pro_specs, wts["w_lat"], _epi_latent,
        [wts["g_kv"], wts["g_kr2"], wts["g_kr2p"], c4, s4],
        [pl.BlockSpec((1, KV_LORA), lambda i, j: (0, 0)), lane_vec, lane_vec, tab_spec, tab_spec],
        [jax.ShapeDtypeStruct((m, LATENT), F32), jax.ShapeDtypeStruct((m, KV_LORA), BF16),
         jax.ShapeDtypeStruct((m, LANE), BF16)],
        [pl.BlockSpec((tm, LATENT), lambda i, j: (i, 0)), pl.BlockSpec((tm, KV_LORA), lambda i, j: (i, 0)),
         pl.BlockSpec((tm, LANE), lambda i, j: (i, 0))],
        tm, n_lat)
    gate = _mm_simple("mla_gate", _pro_rms, pro_args, ["row", "vec"], wts["w_gate"])
    return qn, qr, lat, cn, kr, gate


def _mla_prompt(x, g_norm, wts, c4, s4):
    b, s, d = x.shape
    x2 = x.reshape(b * s, d)
    tm = _pick_tm(b * s)
    nsb = s // tm
    qn, qr, lat, cn, kr, gate = _mla_project(x2, g_norm, wts, c4, s4, s)
    head_shape = [jax.ShapeDtypeStruct((b, HEADS, s, LANE), BF16)]
    head_spec = [pl.BlockSpec((1, 4, tm, LANE), lambda i, j: (i // nsb, j, i % nsb, 0))]
    kn = _mm("mla_knope", _pro_cast, [cn], [_row_spec(tm, KV_LORA)], wts["wuk"], _epi_knope, [wts["g_kn"]],
             [pl.BlockSpec((1, LANE), lambda i, j: (0, 0))], head_shape, head_spec, tm, 512)[0]
    v = _mm("mla_value", _pro_cast, [cn], [_row_spec(tm, KV_LORA)], wts["wuv"], _epi_heads, [], [],
            head_shape, head_spec, tm, 512)[0]
    o = _flash_attention(qn, qr, kn, kr.reshape(b, s, LANE), v).reshape(b * s, HEADS * D_NOPE)
    y = _mm_simple("mla_out", _pro_gate, [o, gate], ["row", "row"], wts["w_o"], _epi_res, [x2], ["tile"])
    return y.reshape(b, s, d), lat.reshape(b, s, LATENT)


def _mla_sample(x, g_norm, wts, c4, s4, cache, page_table):
    b, t, d = x.shape
    m = b * t
    x2 = x.reshape(m, d)
    qn, qr, lat, cn, kr, gate = _mla_project(x2, g_norm, wts, c4, s4, None)
    qp = _absorb_q(qn, wts["g_kn"], wts["wuk"])
    qp = qp.reshape(HEADS, b, t, KV_LORA).transpose(1, 2, 0, 3).reshape(b, t * HEADS, KV_LORA)
    qr = qr.reshape(b, t * HEADS, D_ROPE).astype(BF16)
    cnew = jnp.pad(cn.reshape(b, t, KV_LORA), ((0, 0), (0, PAGE - t), (0, 0)))
    krnew = jnp.pad(kr[:, :D_ROPE].reshape(b, t, D_ROPE), ((0, 0), (0, PAGE - t), (0, 0)))
    o = _dec_attention(cache, page_table, wts["wuk_t"], qp, qr, cnew, krnew, wts["wuv"])
    o = o.reshape(m, HEADS * D_NOPE)
    y = _mm_simple("mla_out_s", _pro_gate, [o, gate], ["row", "row"], wts["w_o"], _epi_res, [x2], ["tile"])
    return y.reshape(b, t, d), lat.reshape(b, t, LATENT)


def _s5_body(u_ref, wb_ref, wc_ref, a_ref, apow_ref, d_ref, x0_ref, g_ref, xl_ref, bu_ref, carry_ref,
             *, seq, rows_per_seq_tile):
    tc = u_ref.shape[0]
    ns = S5_GB * S5_STATE
    long_seq = seq >= tc
    u = u_ref[...]
    bu_ref[...] = jnp.dot(u.astype(BF16), wb_ref[0], preferred_element_type=F32)
    a_re, a_im = a_ref[0, 0:1, :], a_ref[0, 1:2, :]

    def cmul_add(xr, xi, pr, pi, sr, si):
        return xr + pr * sr - pi * si, xi + pr * si + pi * sr

    if long_seq:
        t_idx = pl.program_id(1) % (seq // tc)

        @pl.when(t_idx == 0)
        def _():
            carry_ref[...] = x0_ref[0]

        row = lax.broadcasted_iota(jnp.int32, (8, ns), 0)
        pw = [(apow_ref[0, 0, k - 1:k, :], apow_ref[0, 1, k - 1:k, :]) for k in (1, 2, 4)]
        ap_re, ap_im = apow_ref[0, 0], apow_ref[0, 1]

        def tile_step(j, carry):
            cr, ci = carry
            r0 = pl.multiple_of(j * 8, 8)
            xr = bu_ref[pl.ds(r0, 8), 0:ns]
            xi = bu_ref[pl.ds(r0, 8), ns:2 * ns]
            for k, (pr, pi) in zip((1, 2, 4), pw):
                keep = row >= k
                sr = jnp.where(keep, pltpu.roll(xr, k, 0), 0.0)
                si = jnp.where(keep, pltpu.roll(xi, k, 0), 0.0)
                xr, xi = cmul_add(xr, xi, pr, pi, sr, si)
            xr, xi = cmul_add(xr, xi, ap_re, ap_im, cr, ci)
            bu_ref[pl.ds(r0, 8), 0:ns] = xr
            bu_ref[pl.ds(r0, 8), ns:2 * ns] = xi
            return xr[7:8, :], xi[7:8, :]

        cr, ci = lax.fori_loop(0, tc // 8, tile_step, (carry_ref[0:1, :], carry_ref[1:2, :]))
        carry_ref[0:1, :] = cr
        carry_ref[1:2, :] = ci
        xl_ref[0, 0:1, :] = cr
        xl_ref[0, 1:2, :] = ci
    else:
        xr = bu_ref[:, 0:ns]
        xi = bu_ref[:, ns:2 * ns]
        pos = lax.broadcasted_iota(jnp.int32, (tc, ns), 0) % seq
        k = 1
        pr, pi = a_re, a_im
        while k < seq:
            keep = pos >= k
            sr = jnp.where(keep, pltpu.roll(xr, k, 0), 0.0)
            si = jnp.where(keep, pltpu.roll(xi, k, 0), 0.0)
            xr, xi = cmul_add(xr, xi, pr, pi, sr, si)
            pr, pi = pr * pr - pi * pi, 2.0 * pr * pi
            k *= 2
        ap_re = jnp.concatenate([apow_ref[0, 0, 0:seq, :]] * (tc // seq), axis=0)
        ap_im = jnp.concatenate([apow_ref[0, 1, 0:seq, :]] * (tc // seq), axis=0)
        xr, xi = cmul_add(xr, xi, ap_re, ap_im, x0_ref[:, 0:ns], x0_ref[:, ns:2 * ns])
        bu_ref[:, 0:ns] = xr
        bu_ref[:, ns:2 * ns] = xi
        xl_ref[...] = bu_ref[...]

    y = jnp.dot(bu_ref[...].astype(BF16), wc_ref[0], preferred_element_type=F32) + d_ref[...] * u
    g_ref[...] = _gelu_tanh(y)


def _s5_discretize(lam_re, lam_im, log_dt, b_re, b_im, c_re, c_im):
    g, n = lam_re.shape
    dt = jnp.exp(log_dt)[:, None]
    mag = jnp.exp(lam_re * dt)
    ab_re, ab_im = mag * jnp.cos(lam_im * dt), mag * jnp.sin(lam_im * dt)
    nr, ni = ab_re - 1.0, ab_im
    den = lam_re * lam_re + lam_im * lam_im
    fr, fi = (nr * lam_re + ni * lam_im) / den, (ni * lam_re - nr * lam_im) / den
    bb_re = fr[..., None] * b_re - fi[..., None] * b_im
    bb_im = fr[..., None] * b_im + fi[..., None] * b_re
    nblk = g // S5_GB
    eye = jnp.eye(S5_GB, dtype=F32)

    def expand_b(bb):
        bb = bb.reshape(nblk, S5_GB, n, S5_GROUP)
        return jnp.einsum("bgnj,gh->bgjhn", bb, eye).reshape(nblk, S5_GB * S5_GROUP, S5_GB * n)

    def expand_c(cc):
        cc = cc.reshape(nblk, S5_GB, S5_GROUP, n)
        return jnp.einsum("bgjn,gh->bgnhj", cc, eye).reshape(nblk, S5_GB * n, S5_GB * S5_GROUP)

    wb = jnp.concatenate([expand_b(bb_re), expand_b(bb_im)], axis=2).astype(BF16)
    wc = jnp.concatenate([expand_c(c_re), -expand_c(c_im)], axis=1).astype(BF16)
    kk = jnp.arange(1, 9, dtype=F32)[:, None, None]
    magk = jnp.exp(kk * lam_re * dt)
    pk_re, pk_im = magk * jnp.cos(kk * lam_im * dt), magk * jnp.sin(kk * lam_im * dt)
    apow = jnp.stack([pk_re, pk_im], axis=0).reshape(2, 8, nblk, S5_GB * n).transpose(2, 0, 1, 3)
    a1 = jnp.stack([ab_re, ab_im], axis=0).reshape(2, nblk, S5_GB * n).transpose(1, 0, 2)
    return wb, wc, a1, apow


def _s5_mix(u_src, col0, x0_re, x0_im, prm, seq):
    wb, wc, a1, apow, d_skip = prm
    m = u_src.shape[0]
    nblk = wb.shape[0]
    cb = S5_GB * S5_GROUP
    ns = S5_GB * S5_STATE
    width = nblk * cb
    nseq = m // seq
    tc = _pick_tm(m) if seq >= 8 else m
    long_seq = seq >= tc
    n_tiles = m // tc
    cblk0 = col0 // cb
    x0 = jnp.concatenate([x0_re.reshape(nseq, nblk, ns), x0_im.reshape(nseq, nblk, ns)], axis=2)
    if long_seq:
        tiles_per_seq = seq // tc
        x0_arr = x0.transpose(1, 0, 2).reshape(nblk * nseq, 2, ns)
        x0_spec = pl.BlockSpec((1, 2, ns), lambda gb, r: (gb * nseq + r // tiles_per_seq, 0, 0))
        xl_shape = jax.ShapeDtypeStruct((nblk * nseq, 2, ns), F32)
        xl_spec = pl.BlockSpec((1, 2, ns), lambda gb, r: (gb * nseq + r // tiles_per_seq, 0, 0))
        carry_shape = (2, ns)
    else:
        x0_rows = jnp.repeat(x0, seq, axis=0)
        x0_arr = x0_rows.reshape(m, nblk * 2 * ns)
        x0_spec = pl.BlockSpec((tc, 2 * ns), lambda gb, r: (r, gb))
        xl_shape = jax.ShapeDtypeStruct((m, nblk * 2 * ns), F32)
        xl_spec = pl.BlockSpec((tc, 2 * ns), lambda gb, r: (r, gb))
        carry_shape = (2, ns)
    body = functools.partial(_s5_body, seq=seq, rows_per_seq_tile=tc)
    g, xl = pl.pallas_call(
        body,
        grid=(nblk, n_tiles),
        in_specs=[
            pl.BlockSpec((tc, cb), lambda gb, r: (r, cblk0 + gb)),
            pl.BlockSpec((1, cb, 2 * ns), lambda gb, r: (gb, 0, 0)),
            pl.BlockSpec((1, 2 * ns, cb), lambda gb, r: (gb, 0, 0)),
            pl.BlockSpec((1, 2, ns), lambda gb, r: (gb, 0, 0)),
            pl.BlockSpec((1, 2, 8, ns), lambda gb, r: (gb, 0, 0, 0)),
            pl.BlockSpec((1, cb), lambda gb, r: (0, gb)),
            x0_spec,
        ],
        out_specs=[pl.BlockSpec((tc, cb), lambda gb, r: (r, gb)), xl_spec],
        out_shape=[jax.ShapeDtypeStruct((m, width), F32), xl_shape],
        scratch_shapes=[pltpu.VMEM((tc, 2 * ns), F32), pltpu.VMEM(carry_shape, F32)],
        compiler_params=_cparams(("parallel", "arbitrary")),
        name="s5_mix",
    )(u_src, wb, wc, a1, apow, d_skip, x0_arr)
    if long_seq:
        xl = xl.reshape(nblk, nseq, 2, ns).transpose(2, 1, 0, 3).reshape(2, nseq, nblk * ns)
    else:
        xl = xl.reshape(nseq, seq, nblk, 2, ns)[:, seq - 1].transpose(2, 0, 1, 3).reshape(2, nseq, nblk * ns)
    return g, xl[0], xl[1]


def _s5_layer(x, g_norm, x0_re, x0_im, w_in, mix_prm, w_glu, b_glu, w_o):
    b, t, d = x.shape
    m = b * t
    x2 = x.reshape(m, d)
    width = w_glu.shape[0]
    z = _mm_simple("s5_in", _pro_rms, [x2, g_norm], ["row", "vec"], w_in)
    g, xr, xi = _s5_mix(z, 0, x0_re, x0_im, mix_prm, t)
    g2 = _mm_simple("s5_glu", _pro_cast, [g], ["row"], w_glu, _epi_glu, [g, b_glu], ["tile", "nvec"])
    tm = _pick_tm(m)
    gate_spec = pl.BlockSpec((tm, width), lambda i, j: (i, 1))
    y = _mm("s5_out", _pro_gate, [g2, z], [_row_spec(tm, width), gate_spec], w_o, _epi_res, [x2],
            [_tile_spec(tm, 512)], [jax.ShapeDtypeStruct((m, d), F32)], [_tile_spec(tm, 512)], tm, 512)[0]
    ng = width // S5_GROUP
    return y.reshape(b, t, d), xr.reshape(b, ng, S5_STATE), xi.reshape(b, ng, S5_STATE)


def _rwkv_body(r_ref, k_ref, v_ref, dw_ref, da_ref, w0_ref, a0_ref, kk_ref, ka_ref, rk_ref, gw_ref, gb_ref,
               s0_ref, y_ref, sout_ref, state_ref, *, t_valid, n_chunks, heads_per_blk):
    lc = r_ref.shape[1]
    ci = pl.program_id(2)

    @pl.when(ci == 0)
    def _():
        state_ref[...] = s0_ref[0]

    row = lax.broadcasted_iota(jnp.int32, (lc, lc), 0)
    col = lax.broadcasted_iota(jnp.int32, (lc, lc), 1)
    tri_incl = (col <= row).astype(F32)
    strict = col < row
    incl = col <= row
    eye = (col == row).astype(F32)
    valid = None
    if t_valid < lc:
        valid = lax.broadcasted_iota(jnp.int32, (lc, RW_HEAD), 0) < t_valid

    outs = []
    for hh in range(heads_per_blk):
        sl = slice(hh * RW_HEAD, (hh + 1) * RW_HEAD)
        r = r_ref[0, :, sl]
        k = k_ref[0, :, sl]
        v = v_ref[0, :, sl]
        wl = -_softplus(-(w0_ref[:, sl] + dw_ref[0, :, sl])) - 0.5
        lw = -jnp.exp(wl)
        a_rate = _sigmoid(a0_ref[:, sl] + da_ref[0, :, sl])
        kkr = k * kk_ref[:, sl]
        kk = kkr / jnp.maximum(jnp.sqrt(jnp.sum(kkr * kkr, axis=-1, keepdims=True)), 1e-12)
        kf = k * (1.0 + (a_rate - 1.0) * ka_ref[:, sl])
        av = -kk
        bv = kk * a_rate
        if valid is not None:
            lw = jnp.where(valid, lw, 0.0)
            av = jnp.where(valid, av, 0.0)
            bv = jnp.where(valid, bv, 0.0)
            kf = jnp.where(valid, kf, 0.0)
        cl = jnp.dot(tri_incl, lw, preferred_element_type=F32, precision=lax.Precision.HIGHEST)
        p_in = jnp.exp(cl)
        p_ex = jnp.exp(cl - lw)
        p_inv = jnp.exp(-cl)
        p_end = jnp.exp(cl[lc - 1:lc, :] - cl)
        at = (av * p_ex).astype(BF16)
        rt = (r * p_in).astype(BF16)
        bt = (bv * p_inv).astype(BF16)
        kt = (kf * p_inv).astype(BF16)
        lhs = jnp.concatenate([at, rt], axis=0)
        rhs = jnp.concatenate([bt, kt], axis=0)
        amat = lax.dot_general(lhs, rhs, _NT, preferred_element_type=F32)
        a_ab = jnp.where(strict, amat[:lc, :lc], 0.0)
        a_ak = jnp.where(strict, amat[:lc, lc:], 0.0)
        a_rb = jnp.where(incl, amat[lc:, :lc], 0.0)
        a_rk = jnp.where(incl, amat[lc:, lc:], 0.0)
        hp = lax.Precision.HIGHEST
        tinv = eye + a_ab
        npow = a_ab
        span = 2
        while span < lc:
            npow = jnp.dot(npow, npow, preferred_element_type=F32, precision=hp)
            tinv = tinv + jnp.dot(npow, tinv, preferred_element_type=F32, precision=hp)
            span *= 2
        s0 = state_ref[hh]
        s0b = s0.astype(BF16)
        vb = v.astype(BF16)
        rhs_u = lax.dot_general(at, s0b, _NT, preferred_element_type=F32)
        rhs_u = rhs_u + jnp.dot(a_ak.astype(BF16), vb, preferred_element_type=F32)
        u = jnp.dot(tinv, rhs_u, preferred_element_type=F32, precision=hp)
        ub = u.astype(BF16)
        y = lax.dot_general(rt, s0b, _NT, preferred_element_type=F32)
        y = y + jnp.dot(a_rb.astype(BF16), ub, preferred_element_type=F32)
        y = y + jnp.dot(a_rk.astype(BF16), vb, preferred_element_type=F32)
        s_new = s0 * p_in[lc - 1:lc, :]
        s_new = s_new + lax.dot_general(ub, (bv * p_end).astype(BF16), _TN, preferred_element_type=F32)
        s_new = s_new + lax.dot_general(vb, (kf * p_end).astype(BF16), _TN, preferred_element_type=F32)
        state_ref[hh] = s_new
        mean = jnp.mean(y, axis=-1, keepdims=True)
        yc = y - mean
        var = jnp.mean(yc * yc, axis=-1, keepdims=True)
        yn = yc * lax.rsqrt(var + GN_EPS) * gw_ref[:, sl] + gb_ref[:, sl]
        bonus = jnp.sum(r * kf * rk_ref[:, sl], axis=-1, keepdims=True) * v
        outs.append(yn + bonus)

    y_ref[0] = jnp.concatenate(outs, axis=-1)

    @pl.when(ci == n_chunks - 1)
    def _():
        sout_ref[0] = state_ref[...]


def _rwkv_core(r, k, v, dw, da, vecs, s0, t_valid):
    b, t, w = r.shape
    lc = min(RW_CHUNK, t)
    n_chunks = t // lc
    hpb = 8
    wb = hpb * RW_HEAD
    nhb = w // wb
    seq_spec = pl.BlockSpec((1, lc, wb), lambda bi, hi, ci: (bi, ci, hi))
    vec_spec = pl.BlockSpec((1, wb), lambda bi, hi, ci: (0, hi))
    st_spec = pl.BlockSpec((1, hpb, RW_HEAD, RW_HEAD), lambda bi, hi, ci: (bi, hi, 0, 0))
    body = functools.partial(_rwkv_body, t_valid=t_valid, n_chunks=n_chunks, heads_per_blk=hpb)
    return pl.pallas_call(
        body,
        grid=(b, nhb, n_chunks),
        in_specs=[seq_spec] * 5 + [vec_spec] * 7 + [st_spec],
        out_specs=[seq_spec, st_spec],
        out_shape=[jax.ShapeDtypeStruct((b, t, w), F32), jax.ShapeDtypeStruct(s0.shape, F32)],
        scratch_shapes=[pltpu.VMEM((hpb, RW_HEAD, RW_HEAD), F32)],
        compiler_params=_cparams(("parallel", "parallel", "arbitrary")),
        name="rwkv7_core",
    )(r, k, v, dw, da, *vecs, s0)


def _rwkv_layer(x, g_norm, shift0, s0, wts):
    b, t, d = x.shape
    m = b * t
    x2 = x.reshape(m, d)
    h = _rms_rows(x2, g_norm)
    h3 = h.reshape(b, t, d)
    prev = jnp.concatenate([shift0[:, None, :], h3[:, :-1]], axis=1).reshape(m, d)
    mu = wts["mu"]

    def mixed(name, n, w):
        return _mm_simple(name, _pro_mix, [h, prev, mu[n:n + 1]], ["row", "row", "vec"], w)

    r = mixed("rwkv_r", 0, wts["w_r"])
    k = mixed("rwkv_k", 2, wts["w_k"])
    v = mixed("rwkv_v", 3, wts["w_v"])
    gpre = mixed("rwkv_g", 5, wts["w_g"])
    dw = _mm_simple("rwkv_w2", _pro_tanh, [mixed("rwkv_w1", 1, wts["w1"])], ["row"], wts["w2"])
    da = _mm_simple("rwkv_a2", _pro_cast, [mixed("rwkv_a1", 4, wts["a1"])], ["row"], wts["a2"])
    tp = t if t % 8 == 0 else 8 * ((t + 7) // 8)

    def seq3(a):
        a = a.reshape(b, t, -1)
        return a if tp == t else jnp.pad(a, ((0, 0), (0, tp - t), (0, 0)))

    y, s_fin = _rwkv_core(seq3(r), seq3(k), seq3(v), seq3(dw), seq3(da), wts["vecs"], s0, t)
    y = y[:, :t].reshape(m, -1)
    out = _mm_simple("rwkv_out", _pro_gate, [y, gpre], ["row", "row"], wts["w_o"], _epi_res, [x2], ["tile"])
    return out.reshape(b, t, d), h3[:, -1], s_fin


def _rwkv_weights(mu, w_r, w_k, w_v, w_g, w_o, w0, w1, w2, a0, a1, a2, k_k, k_a, r_k, gn_w, gn_b):
    lora = w1.shape[1]
    lpad = LANE * ((lora + LANE - 1) // LANE)

    def pad_cols(w):
        return jnp.pad(w, ((0, 0), (0, lpad - lora))).astype(BF16)

    def pad_rows(w):
        return jnp.pad(w, ((0, lpad - lora), (0, 0))).astype(BF16)

    row = lambda a: a.reshape(1, -1)
    return dict(
        mu=mu, w_r=w_r.astype(BF16), w_k=w_k.astype(BF16), w_v=w_v.astype(BF16), w_g=w_g.astype(BF16),
        w_o=w_o.astype(BF16), w1=pad_cols(w1), w2=pad_rows(w2), a1=pad_cols(a1), a2=pad_rows(a2),
        vecs=[row(w0), row(a0), row(k_k), row(k_a), row(r_k), row(gn_w), row(gn_b)],
    )


def kernel(x_prompt, x_sample, cache_mla, page_table, state_s5_re, state_s5_im, state_rwkv, state_rwkv_shift, norm_g, mla_w_in, mla_g_qn, mla_g_qr, mla_g_kv, mla_g_kr, mla_g_kn, mla_w_uk, mla_w_uv, mla_w_o, s5_w_in, s5_lam_re, s5_lam_im, s5_log_dt, s5_b_re, s5_b_im, s5_c_re, s5_c_im, s5_d, s5_w_glu, s5_b_glu, s5_w_o, rwkv_mu, rwkv_w_r, rwkv_w_k, rwkv_w_v, rwkv_w_g, rwkv_w_o, rwkv_w0, rwkv_w1, rwkv_w2, rwkv_a0, rwkv_a1, rwkv_a2, rwkv_k_k, rwkv_k_a, rwkv_r_k, rwkv_gn_w, rwkv_gn_b):
    depth = norm_g.shape[0]
    nb, seq, d = x_prompt.shape
    db, t_dec, _ = x_sample.shape
    past = page_table.shape[1] * PAGE
    c4p, s4p = _rope_tables(jnp.arange(seq, dtype=jnp.int32))
    c4s, s4s = _rope_tables(past + jnp.arange(t_dec, dtype=jnp.int32))
    c4s, s4s = jnp.tile(c4s, (db, 1)), jnp.tile(s4s, (db, 1))
    yp, ys = x_prompt, x_sample
    mla_p, mla_s = [], []
    s5r_p, s5i_p, s5r_s, s5i_s = [], [], [], []
    rw_p, rw_s, sh_p, sh_s = [], [], [], []
    for i in range(depth):
        kind, j = i % 3, i // 3
        g_norm = norm_g[i].reshape(1, d)
        if kind == 0:
            wts = _mla_weights(mla_w_in[j], mla_g_qn[j], mla_g_qr[j], mla_g_kv[j], mla_g_kr[j], mla_g_kn[j],
                               mla_w_uk[j], mla_w_uv[j], mla_w_o[j])
            yp, lat = _mla_prompt(yp, g_norm, wts, c4p, s4p)
            mla_p.append(lat)
            ys, lat = _mla_sample(ys, g_norm, wts, c4s, s4s, cache_mla[j], page_table)
            mla_s.append(lat)
        elif kind == 1:
            wb, wc, a1, apow = _s5_discretize(s5_lam_re[j], s5_lam_im[j], s5_log_dt[j], s5_b_re[j], s5_b_im[j],
                                              s5_c_re[j], s5_c_im[j])
            prm = (wb, wc, a1, apow, s5_d[j].reshape(1, -1))
            args = (s5_w_in[j].astype(BF16), prm, s5_w_glu[j].astype(BF16), s5_b_glu[j].reshape(1, -1),
                    s5_w_o[j].astype(BF16))
            ng, ns = s5_lam_re.shape[1], s5_lam_re.shape[2]
            zero = jnp.zeros((nb, ng, ns), F32)
            yp, xr, xi = _s5_layer(yp, g_norm, zero, zero, *args)
            s5r_p.append(xr)
            s5i_p.append(xi)
            ys, xr, xi = _s5_layer(ys, g_norm, state_s5_re[j], state_s5_im[j], *args)
            s5r_s.append(xr)
            s5i_s.append(xi)
        else:
            wts = _rwkv_weights(rwkv_mu[j], rwkv_w_r[j], rwkv_w_k[j], rwkv_w_v[j], rwkv_w_g[j], rwkv_w_o[j],
                                rwkv_w0[j], rwkv_w1[j], rwkv_w2[j], rwkv_a0[j], rwkv_a1[j], rwkv_a2[j],
                                rwkv_k_k[j], rwkv_k_a[j], rwkv_r_k[j].reshape(-1), rwkv_gn_w[j], rwkv_gn_b[j])
            nh = rwkv_r_k.shape[1]
            sh0 = jnp.zeros((nb, d), F32)
            st0 = jnp.zeros((nb, nh, RW_HEAD, RW_HEAD), F32)
            yp, sh, st = _rwkv_layer(yp, g_norm, sh0, st0, wts)
            sh_p.append(sh)
            rw_p.append(st)
            ys, sh, st = _rwkv_layer(ys, g_norm, state_rwkv_shift[j], state_rwkv[j], wts)
            sh_s.append(sh)
            rw_s.append(st)
    return (yp, ys, jnp.stack(mla_p), jnp.stack(mla_s),
            jnp.stack(s5r_p), jnp.stack(s5i_p), jnp.stack(s5r_s), jnp.stack(s5i_s),
            jnp.stack(rw_p), jnp.stack(rw_s), jnp.stack(sh_p), jnp.stack(sh_s))
```

```python
import functools
import math

import jax
import jax.numpy as jnp
from jax import lax
from jax.experimental import pallas as pl
from jax.experimental.pallas import tpu as pltpu

F32 = jnp.float32
BF16 = jnp.bfloat16

EPS = 1e-6
PAGE = 128
HEADS = 16
D_NOPE = 128
D_ROPE = 64
D_QK = D_NOPE + D_ROPE
KV_LORA = 512
LATENT = KV_LORA + D_ROPE
ROPE_THETA = 10000.0
SM_SCALE = D_QK ** -0.5
NEG = -1e30
S5_GROUP = 16
S5_STATE = 64
S5_GB = 16
RW_HEAD = 64
RW_CHUNK = 64
GN_EPS = 64e-5
DEC_GROUP = 16
LANE = 128
VMEM_LIMIT = 48 * 1024 * 1024

_NT = (((1,), (1,)), ((), ()))
_TN = (((0,), (0,)), ((), ()))


def _cparams(sem):
    return pltpu.CompilerParams(dimension_semantics=sem, vmem_limit_bytes=VMEM_LIMIT)


def _sigmoid(x):
    return 1.0 / (1.0 + jnp.exp(-x))


def _silu(x):
    return x * _sigmoid(x)


def _gelu_tanh(x):
    return 0.5 * x * (1.0 + jnp.tanh(0.7978845608028654 * (x + 0.044715 * x * x * x)))


def _softplus(x):
    return jnp.maximum(x, 0.0) + jnp.log(1.0 + jnp.exp(-jnp.abs(x)))


def _mm_body(*refs, pro, n_pro, epi, n_epi, n_out):
    pro_refs = refs[:n_pro]
    w_ref = refs[n_pro]
    epi_refs = refs[n_pro + 1:n_pro + 1 + n_epi]
    out_refs = refs[n_pro + 1 + n_epi:n_pro + 1 + n_epi + n_out]
    xs_ref = refs[-1]

    @pl.when(pl.program_id(1) == 0)
    def _():
        xs_ref[...] = pro(*[r[...] for r in pro_refs]).astype(BF16)

    acc = jnp.dot(xs_ref[...], w_ref[pl.program_id(1)], preferred_element_type=F32)
    epi(acc, epi_refs, out_refs)


def _mm(name, pro, pro_args, pro_specs, w, epi, epi_args, epi_specs, out_shapes, out_specs, tm, tn):
    m = pro_args[0].shape[0]
    k, n = w.shape
    assert m % tm == 0 and n % tn == 0, (name, m, tm, n, tn)
    body = functools.partial(_mm_body, pro=pro, n_pro=len(pro_args), epi=epi, n_epi=len(epi_args),
                             n_out=len(out_shapes))
    w = w.reshape(k, n // tn, tn).transpose(1, 0, 2)
    w_spec = pl.BlockSpec((n // tn, k, tn), lambda i, j: (0, 0, 0), pipeline_mode=pl.Buffered(1))
    return pl.pallas_call(
        body,
        grid=(m // tm, n // tn),
        in_specs=[*pro_specs, w_spec, *epi_specs],
        out_specs=out_specs,
        out_shape=out_shapes,
        scratch_shapes=[pltpu.VMEM((tm, k), BF16)],
        compiler_params=_cparams(("parallel", "arbitrary")),
        name=name,
    )(*pro_args, w, *epi_args)


def _row_spec(tm, k):
    return pl.BlockSpec((tm, k), lambda i, j: (i, 0))


def _vec_spec(k):
    return pl.BlockSpec((1, k), lambda i, j: (0, 0))


def _tile_spec(tm, tn):
    return pl.BlockSpec((tm, tn), lambda i, j: (i, j))


def _nvec_spec(tn):
    return pl.BlockSpec((1, tn), lambda i, j: (0, j))


def _pro_rms(x, g):
    return x * lax.rsqrt(jnp.mean(x * x, axis=-1, keepdims=True) + EPS) * g


def _pro_gate(a, b):
    return a * _silu(b)


def _pro_mix(h, prev, mu):
    return h + (prev - h) * mu


def _pro_cast(x):
    return x


def _pro_tanh(x):
    return jnp.tanh(x)


def _epi_plain(acc, epi_refs, out_refs):
    out_refs[0][...] = acc.astype(out_refs[0].dtype)


def _epi_res(acc, epi_refs, out_refs):
    out_refs[0][...] = epi_refs[0][...] + acc


def _epi_glu(acc, epi_refs, out_refs):
    g_ref, b_ref = epi_refs
    out_refs[0][...] = g_ref[...] * _sigmoid(acc + b_ref[...])


def _pick_tm(m):
    return 512 if m % 512 == 0 else m


def _pick_tn(n):
    for t in (512, 256, 128):
        if n % t == 0:
            return t
    return n


def _mm_simple(name, pro, pro_args, pro_kinds, w, epi=_epi_plain, epi_args=(), epi_kinds=(),
               out_dtype=F32):
    m = pro_args[0].shape[0]
    k, n = w.shape
    tm, tn = _pick_tm(m), _pick_tn(n)
    pro_specs = [_row_spec(tm, k) if kd == "row" else _vec_spec(k) for kd in pro_kinds]
    epi_specs = [_tile_spec(tm, tn) if kd == "tile" else _nvec_spec(tn) for kd in epi_kinds]
    return _mm(name, pro, list(pro_args), pro_specs, w, epi, list(epi_args), epi_specs,
               [jax.ShapeDtypeStruct((m, n), out_dtype)], [_tile_spec(tm, tn)], tm, tn)[0]


def _rms_body(x_ref, g_ref, o_ref):
    o_ref[...] = _pro_rms(x_ref[...], g_ref[...])


def _rms_rows(x, g):
    m, k = x.shape
    tm = _pick_tm(m)
    return pl.pallas_call(
        _rms_body,
        grid=(m // tm,),
        in_specs=[pl.BlockSpec((tm, k), lambda i: (i, 0)), pl.BlockSpec((1, k), lambda i: (0, 0))],
        out_specs=pl.BlockSpec((tm, k), lambda i: (i, 0)),
        out_shape=jax.ShapeDtypeStruct((m, k), F32),
        compiler_params=_cparams(("parallel",)),
        name="rms_rows",
    )(x, g)


def _epi_qnope(acc, epi_refs, out_refs, *, four_d):
    g_ref, = epi_refs
    o_ref, = out_refs
    for hh in range(acc.shape[1] // LANE):
        a = acc[:, hh * LANE:(hh + 1) * LANE]
        y = a * lax.rsqrt(jnp.mean(a * a, axis=-1, keepdims=True) + EPS) * g_ref[...] * SM_SCALE
        if four_d:
            o_ref[0, hh] = y.astype(o_ref.dtype)
        else:
            o_ref[:, hh * LANE:(hh + 1) * LANE] = y.astype(o_ref.dtype)


def _rope_pair_tile(x, g2, c4, s4):
    lane = lax.broadcasted_iota(jnp.int32, x.shape, 1)
    lo = lane < D_ROPE
    x2 = x * x
    ss_e = jnp.sum(jnp.where(lo, x2, 0.0), axis=-1, keepdims=True)
    ss_o = jnp.sum(jnp.where(lo, 0.0, x2), axis=-1, keepdims=True)
    inv = jnp.where(lo, lax.rsqrt(ss_e * (1.0 / D_ROPE) + EPS), lax.rsqrt(ss_o * (1.0 / D_ROPE) + EPS))
    xn = x * inv * g2
    first_half = (lane % D_ROPE) < (D_ROPE // 2)
    partner = jnp.where(first_half, pltpu.roll(xn, LANE - D_ROPE // 2, 1), pltpu.roll(xn, D_ROPE // 2, 1))
    return (xn * c4 + partner * s4) * SM_SCALE, lo


def _epi_qrope(acc, epi_refs, out_refs, *, four_d):
    g2_ref, c4_ref, s4_ref = epi_refs
    o_ref, = out_refs
    for p in range(acc.shape[1] // LANE):
        r, lo = _rope_pair_tile(acc[:, p * LANE:(p + 1) * LANE], g2_ref[...], c4_ref[...], s4_ref[...])
        if four_d:
            o_ref[0, 2 * p] = jnp.where(lo, r, 0.0).astype(o_ref.dtype)
            o_ref[0, 2 * p + 1] = jnp.where(lo, 0.0, r).astype(o_ref.dtype)
        else:
            o_ref[:, p * LANE:(p + 1) * LANE] = r.astype(o_ref.dtype)


def _epi_latent(acc, epi_refs, out_refs):
    gkv_ref, g2_ref, g2p_ref, c4_ref, s4_ref = epi_refs
    lat_ref, cn_ref, kr_ref = out_refs
    c = acc[:, :KV_LORA]
    cn = c * lax.rsqrt(jnp.mean(c * c, axis=-1, keepdims=True) + EPS) * gkv_ref[...]
    lat_ref[:, :KV_LORA] = cn
    cn_ref[...] = cn.astype(cn_ref.dtype)
    d = acc[:, KV_LORA:KV_LORA + LANE]
    dp = acc[:, KV_LORA + LANE:KV_LORA + 2 * LANE]
    inv = lax.rsqrt(jnp.mean(d * d, axis=-1, keepdims=True) + EPS)
    r = d * inv * g2_ref[...] * c4_ref[...] + dp * inv * g2p_ref[...] * s4_ref[...]
    lat_ref[:, KV_LORA:LATENT] = r[:, :D_ROPE]
    kr_ref[...] = r.astype(kr_ref.dtype)


def _epi_knope(acc, epi_refs, out_refs):
    g_ref, = epi_refs
    o_ref, = out_refs
    for hh in range(acc.shape[1] // LANE):
        a = acc[:, hh * LANE:(hh + 1) * LANE]
        y = a * lax.rsqrt(jnp.mean(a * a, axis=-1, keepdims=True) + EPS) * g_ref[...]
        o_ref[0, hh] = y.astype(o_ref.dtype)


def _epi_heads(acc, epi_refs, out_refs):
    o_ref, = out_refs
    for hh in range(acc.shape[1] // LANE):
        o_ref[0, hh] = acc[:, hh * LANE:(hh + 1) * LANE].astype(o_ref.dtype)


def _flash_body(qt_ref, kt_ref, qn_ref, qr_ref, kn_ref, kr_ref, v_ref, o_ref, m_ref, l_ref, acc_ref, *, hb):
    t = pl.program_id(2)
    qi = qt_ref[t]
    ki = kt_ref[t]
    heads = range(hb)

    @pl.when(ki == 0)
    def _():
        m_ref[...] = jnp.full(m_ref.shape, NEG, F32)
        l_ref[...] = jnp.zeros(l_ref.shape, F32)
        acc_ref[...] = jnp.zeros(acc_ref.shape, F32)

    def step(masked):
        k_rope = kr_ref[0]
        qs = [jnp.concatenate([qn_ref[0, h], qr_ref[0, h]], axis=-1) for h in heads]
        ks = [jnp.concatenate([kn_ref[0, h], k_rope], axis=-1) for h in heads]
        ss = [lax.dot_general(q, k, _NT, preferred_element_type=F32) for q, k in zip(qs, ks)]
        if masked:
            keep = (lax.broadcasted_iota(jnp.int32, ss[0].shape, 1) <= lax.broadcasted_iota(jnp.int32, ss[0].shape, 0))
            ss = [jnp.where(keep, s, NEG) for s in ss]
        m_old = [m_ref[h] for h in heads]
        m_new = [jnp.maximum(mo, jnp.max(s, axis=-1, keepdims=True)) for mo, s in zip(m_old, ss)]
        alpha = [jnp.exp(mo - mn) for mo, mn in zip(m_old, m_new)]
        ps = [jnp.exp(s - mn) for s, mn in zip(ss, m_new)]
        pv = [jnp.dot(p.astype(BF16), v_ref[0, h], preferred_element_type=F32) for h, p in zip(heads, ps)]
        for h in heads:
            l_ref[h] = alpha[h] * l_ref[h] + jnp.sum(ps[h], axis=-1, keepdims=True)
            acc_ref[h] = alpha[h] * acc_ref[h] + pv[h]
            m_ref[h] = m_new[h]

    @pl.when(ki < qi)
    def _():
        step(False)

    @pl.when(ki == qi)
    def _():
        step(True)
        o_ref[0] = jnp.concatenate([acc_ref[h] / l_ref[h] for h in heads], axis=-1)


def _flash_attention(qn, qr, kn, kr, v):
    b, h, s, _ = qn.shape
    tq = min(512, s)
    nq = s // tq
    hb = 2
    pairs = [(qi, ki) for qi in range(nq) for ki in range(qi + 1)]
    qtab = jnp.asarray([p[0] for p in pairs], jnp.int32)
    ktab = jnp.asarray([p[1] for p in pairs], jnp.int32)
    q_spec = pl.BlockSpec((1, hb, tq, LANE), lambda bi, hi, t, qt, kt: (bi, hi, qt[t], 0))
    k_spec = pl.BlockSpec((1, hb, tq, LANE), lambda bi, hi, t, qt, kt: (bi, hi, kt[t], 0))
    grid_spec = pltpu.PrefetchScalarGridSpec(
        num_scalar_prefetch=2,
        grid=(b, h // hb, len(pairs)),
        in_specs=[q_spec, q_spec, k_spec,
                  pl.BlockSpec((1, tq, LANE), lambda bi, hi, t, qt, kt: (bi, kt[t], 0)), k_spec],
        out_specs=pl.BlockSpec((1, tq, hb * LANE), lambda bi, hi, t, qt, kt: (bi, qt[t], hi)),
        scratch_shapes=[pltpu.VMEM((hb, tq, 1), F32), pltpu.VMEM((hb, tq, 1), F32),
                        pltpu.VMEM((hb, tq, LANE), F32)],
    )
    return pl.pallas_call(
        functools.partial(_flash_body, hb=hb),
        grid_spec=grid_spec,
        out_shape=jax.ShapeDtypeStruct((b, s, h * LANE), F32),
        compiler_params=_cparams(("parallel", "parallel", "arbitrary")),
        name="mla_prompt_attn",
    )(qtab, ktab, qn, qr, kn, kr, v)


def _dec_attn_body(pt_ref, *refs, pps, group, n_steps, t_dec):
    cache_refs = refs[:pps]
    wuk_ref, qp_ref, qr_ref, cnew_ref, krnew_ref, wuv_ref = refs[pps:pps + 6]
    o_ref = refs[pps + 6]
    wq_ref, m_ref, l_ref, acc_ref = refs[pps + 7:]
    del pt_ref
    i = pl.program_id(1)
    n_up = HEADS * D_NOPE
    rows = t_dec * HEADS

    @pl.when(i == 0)
    def _():
        wq_ref[0:n_up, :] = wuk_ref[...]
        wq_ref[n_up:n_up + rows, :] = qp_ref[0]
        m_ref[...] = jnp.full(m_ref.shape, NEG, F32)
        l_ref[...] = jnp.zeros(l_ref.shape, F32)
        acc_ref[...] = jnp.zeros(acc_ref.shape, F32)

    def up_project(c_bf):
        return lax.dot_general(wq_ref[...], c_bf, _NT, preferred_element_type=F32)

    def process(res, c_bf, kr_bf, mask):
        n = c_bf.shape[0]
        kn = res[:n_up].reshape(HEADS, D_NOPE, n)
        inv = lax.rsqrt(jnp.sum(kn * kn, axis=1) * (1.0 / D_NOPE) + EPS)
        s = res[n_up:] * jnp.concatenate([inv] * t_dec, axis=0)
        s = s + lax.dot_general(qr_ref[0], kr_bf, _NT, preferred_element_type=F32)
        if mask is not None:
            s = jnp.where(mask, s, NEG)
        m_old = m_ref[...]
        m_new = jnp.maximum(m_old, jnp.max(s, axis=-1, keepdims=True))
        alpha = jnp.exp(m_old - m_new)
        p = jnp.exp(s - m_new)
        l_ref[...] = alpha * l_ref[...] + jnp.sum(p, axis=-1, keepdims=True)
        acc_ref[...] = alpha * acc_ref[...] + jnp.dot(p.astype(BF16), c_bf, preferred_element_type=F32)
        m_ref[...] = m_new

    staged = []
    for u in range(0, pps, group):
        lats = [cache_refs[u + g][0, 0] for g in range(group)]
        c_bf = jnp.concatenate([lt[:, :KV_LORA].astype(BF16) for lt in lats], axis=0)
        kr_bf = jnp.concatenate([lt[:, KV_LORA:LATENT].astype(BF16) for lt in lats], axis=0)
        staged.append((up_project(c_bf), c_bf, kr_bf))
    for res, c_bf, kr_bf in staged:
        process(res, c_bf, kr_bf, None)

    @pl.when(i == n_steps - 1)
    def _():
        npad = cnew_ref.shape[1]
        key = lax.broadcasted_iota(jnp.int32, (rows, npad), 1)
        tq = lax.broadcasted_iota(jnp.int32, (rows, npad), 0) // HEADS
        process(up_project(cnew_ref[0]), cnew_ref[0], krnew_ref[0], key <= tq)
        o_lat = (acc_ref[...] / l_ref[...]).astype(BF16)
        full = jnp.dot(o_lat, wuv_ref[...], preferred_element_type=F32)
        rh = lax.broadcasted_iota(jnp.int32, full.shape, 0) % HEADS
        ch = lax.broadcasted_iota(jnp.int32, full.shape, 1) // LANE
        diag = jnp.where(rh == ch, full, 0.0).reshape(t_dec, HEADS, full.shape[1])
        o_ref[0] = jnp.sum(diag, axis=1)


def _dec_attention(cache, page_table, wuk_t, qp, qr, cnew, krnew, wuv, layer):
    bdec, n_pages = page_table.shape
    rows = qp.shape[1]
    t_dec = rows // HEADS
    pps = 16 if n_pages % 16 == 0 else 2
    group = min(pps, DEC_GROUP)
    n_steps = n_pages // pps
    n_up = HEADS * D_NOPE

    def cache_spec(u):
        return pl.BlockSpec((1, 1, PAGE, LATENT), lambda b, i, pt: (layer, pt[b, i * pps + u], 0, 0))

    body = functools.partial(_dec_attn_body, pps=pps, group=group, n_steps=n_steps, t_dec=t_dec)
    grid_spec = pltpu.PrefetchScalarGridSpec(
        num_scalar_prefetch=1,
        grid=(bdec, n_steps),
        in_specs=[
            *[cache_spec(u) for u in range(pps)],
            pl.BlockSpec((n_up, KV_LORA), lambda b, i, pt: (0, 0)),
            pl.BlockSpec((1, rows, KV_LORA), lambda b, i, pt: (b, 0, 0)),
            pl.BlockSpec((1, rows, D_ROPE), lambda b, i, pt: (b, 0, 0)),
            pl.BlockSpec((1, PAGE, KV_LORA), lambda b, i, pt: (b, 0, 0)),
            pl.BlockSpec((1, PAGE, D_ROPE), lambda b, i, pt: (b, 0, 0)),
            pl.BlockSpec((KV_LORA, n_up), lambda b, i, pt: (0, 0)),
        ],
        out_specs=pl.BlockSpec((1, t_dec, n_up), lambda b, i, pt: (b, 0, 0)),
        scratch_shapes=[
            pltpu.VMEM((n_up + rows, KV_LORA), BF16),
            pltpu.VMEM((rows, 1), F32),
            pltpu.VMEM((rows, 1), F32),
            pltpu.VMEM((rows, KV_LORA), F32),
        ],
    )
    return pl.pallas_call(
        body,
        grid_spec=grid_spec,
        out_shape=jax.ShapeDtypeStruct((bdec, t_dec, n_up), F32),
        compiler_params=_cparams(("parallel", "arbitrary")),
        name="mla_sample_attn",
    )(page_table, *([cache] * pps), wuk_t, qp, qr, cnew, krnew, wuv)


def _absorb_body(qn_ref, g_ref, wuk_ref, o_ref):
    q = (qn_ref[...] * g_ref[...]).astype(BF16)
    o_ref[0] = lax.dot_general(q, wuk_ref[...], _NT, preferred_element_type=F32).astype(o_ref.dtype)


def _absorb_q(qn, g_kn, wuk):
    m = qn.shape[0]
    return pl.pallas_call(
        _absorb_body,
        grid=(HEADS,),
        in_specs=[
            pl.BlockSpec((m, LANE), lambda h: (0, h)),
            pl.BlockSpec((1, LANE), lambda h: (0, 0)),
            pl.BlockSpec((KV_LORA, LANE), lambda h: (0, h)),
        ],
        out_specs=pl.BlockSpec((1, m, KV_LORA), lambda h: (h, 0, 0)),
        out_shape=jax.ShapeDtypeStruct((HEADS, m, KV_LORA), BF16),
        compiler_params=_cparams(("parallel",)),
        name="mla_absorb_q",
    )(qn, g_kn, wuk)


def _rope_tables(pos):
    inv = ROPE_THETA ** (-jnp.arange(0, D_ROPE, 2, dtype=F32) / D_ROPE)
    ang = pos.astype(F32)[:, None] * inv[None, :]
    c, s = jnp.cos(ang), jnp.sin(ang)
    return jnp.tile(c, (1, 4)), jnp.concatenate([-s, s, -s, s], axis=1)


def _mla_weights(w_in, g_qn, g_qr, g_kv, g_kr, g_kn, w_uk, w_uv, w_o):
    d = w_in.shape[0]
    q_dim = HEADS * D_QK
    wq = w_in[:, :q_dim].reshape(d, HEADS, D_QK)
    w_kr = w_in[:, q_dim + KV_LORA:q_dim + LATENT]
    w_kr_sw = jnp.concatenate([w_kr[:, D_ROPE // 2:], w_kr[:, :D_ROPE // 2]], axis=1)
    g_kr_sw = jnp.concatenate([g_kr[D_ROPE // 2:], g_kr[:D_ROPE // 2]])
    wuk2 = w_uk.reshape(KV_LORA, HEADS * D_NOPE)
    return dict(
        w_qn=wq[:, :, :D_NOPE].reshape(d, HEADS * D_NOPE).astype(BF16),
        w_qr=wq[:, :, D_NOPE:].reshape(d, HEADS * D_ROPE).astype(BF16),
        w_lat=jnp.concatenate([w_in[:, q_dim:q_dim + KV_LORA], w_kr, w_kr, w_kr_sw, w_kr_sw], axis=1).astype(BF16),
        w_gate=w_in[:, q_dim + LATENT:].astype(BF16),
        g_qn=g_qn.reshape(1, D_NOPE),
        g_qr2=jnp.tile(g_qr, 2).reshape(1, LANE),
        g_kv=g_kv.reshape(1, KV_LORA),
        g_kr2=jnp.tile(g_kr, 2).reshape(1, LANE),
        g_kr2p=jnp.tile(g_kr_sw, 2).reshape(1, LANE),
        g_kn=g_kn.reshape(1, D_NOPE),
        wuk=wuk2.astype(BF16),
        wuk_t=wuk2.T.astype(BF16),
        wuv=w_uv.reshape(KV_LORA, HEADS * D_NOPE).astype(BF16),
        w_o=w_o.astype(BF16),
    )


def _mla_project(x2, g_norm, wts, c4, s4, seq_4d):
    m, d = x2.shape
    tm = _pick_tm(m)
    four_d = seq_4d is not None
    n_tab = c4.shape[0] // tm
    tab_spec = pl.BlockSpec((tm, LANE), lambda i, j: (i % n_tab, 0))
    lane_vec = pl.BlockSpec((1, LANE), lambda i, j: (0, 0))
    pro_args, pro_specs = [x2, g_norm], [_row_spec(tm, d), _vec_spec(d)]

    def q_out(n_heads_blk, n_cols):
        if four_d:
            nsb = seq_4d // tm
            shape = jax.ShapeDtypeStruct((m // seq_4d, HEADS, seq_4d, LANE), BF16)
            spec = pl.BlockSpec((1, n_heads_blk, tm, LANE), lambda i, j: (i // nsb, j, i % nsb, 0))
        else:
            shape = jax.ShapeDtypeStruct((m, n_cols), F32)
            spec = _tile_spec(tm, 512)
        return [shape], [spec]

    shp, spc = q_out(4, HEADS * D_NOPE)
    qn = _mm("mla_qnope", _pro_rms, pro_args, pro_specs, wts["w_qn"],
             functools.partial(_epi_qnope, four_d=four_d), [wts["g_qn"]], [lane_vec], shp, spc, tm, 512)[0]
    shp, spc = q_out(8, HEADS * D_ROPE)
    qr = _mm("mla_qrope", _pro_rms, pro_args, pro_specs, wts["w_qr"],
             functools.partial(_epi_qrope, four_d=four_d), [wts["g_qr2"], c4, s4],
             [lane_vec, tab_spec, tab_spec], shp, spc, tm, 512)[0]
    n_lat = wts["w_lat"].shape[1]
    lat, cn, kr = _mm(
        "mla_latent", _pro_rms, pro_args, pro_specs, wts["w_lat"], _epi_latent,
        [wts["g_kv"], wts["g_kr2"], wts["g_kr2p"], c4, s4],
        [pl.BlockSpec((1, KV_LORA), lambda i, j: (0, 0)), lane_vec, lane_vec, tab_spec, tab_spec],
        [jax.ShapeDtypeStruct((m, LATENT), F32), jax.ShapeDtypeStruct((m, KV_LORA), BF16),
         jax.ShapeDtypeStruct((m, LANE), BF16)],
        [pl.BlockSpec((tm, LATENT), lambda i, j: (i, 0)), pl.BlockSpec((tm, KV_LORA), lambda i, j: (i, 0)),
         pl.BlockSpec((tm, LANE), lambda i, j: (i, 0))],
        tm, n_lat)
    gate = _mm_simple("mla_gate", _pro_rms, pro_args, ["row", "vec"], wts["w_gate"])
    return qn, qr, lat, cn, kr, gate


def _mla_prompt(x, g_norm, wts, c4, s4):
    b, s, d = x.shape
    x2 = x.reshape(b * s, d)
    tm = _pick_tm(b * s)
    nsb = s // tm
    qn, qr, lat, cn, kr, gate = _mla_project(x2, g_norm, wts, c4, s4, s)
    head_shape = [jax.ShapeDtypeStruct((b, HEADS, s, LANE), BF16)]
    head_spec = [pl.BlockSpec((1, 4, tm, LANE), lambda i, j: (i // nsb, j, i % nsb, 0))]
    kn = _mm("mla_knope", _pro_cast, [cn], [_row_spec(tm, KV_LORA)], wts["wuk"], _epi_knope, [wts["g_kn"]],
             [pl.BlockSpec((1, LANE), lambda i, j: (0, 0))], head_shape, head_spec, tm, 512)[0]
    v = _mm("mla_value", _pro_cast, [cn], [_row_spec(tm, KV_LORA)], wts["wuv"], _epi_heads, [], [],
            head_shape, head_spec, tm, 512)[0]
    o = _flash_attention(qn, qr, kn, kr.reshape(b, s, LANE), v).reshape(b * s, HEADS * D_NOPE)
    y = _mm_simple("mla_out", _pro_gate, [o, gate], ["row", "row"], wts["w_o"], _epi_res, [x2], ["tile"])
    return y.reshape(b, s, d), lat.reshape(b, s, LATENT)


def _mla_sample(x, g_norm, wts, c4, s4, cache, page_table, layer):
    b, t, d = x.shape
    m = b * t
    x2 = x.reshape(m, d)
    qn, qr, lat, cn, kr, gate = _mla_project(x2, g_norm, wts, c4, s4, None)
    qp = _absorb_q(qn, wts["g_kn"], wts["wuk"])
    qp = qp.reshape(HEADS, b, t, KV_LORA).transpose(1, 2, 0, 3).reshape(b, t * HEADS, KV_LORA)
    qr = qr.reshape(b, t * HEADS, D_ROPE).astype(BF16)
    cnew = jnp.pad(cn.reshape(b, t, KV_LORA), ((0, 0), (0, PAGE - t), (0, 0)))
    krnew = jnp.pad(kr[:, :D_ROPE].reshape(b, t, D_ROPE), ((0, 0), (0, PAGE - t), (0, 0)))
    o = _dec_attention(cache, page_table, wts["wuk_t"], qp, qr, cnew, krnew, wts["wuv"], layer)
    o = o.reshape(m, HEADS * D_NOPE)
    y = _mm_simple("mla_out_s", _pro_gate, [o, gate], ["row", "row"], wts["w_o"], _epi_res, [x2], ["tile"])
    return y.reshape(b, t, d), lat.reshape(b, t, LATENT)


def _s5_body(u_ref, wb_ref, wc_ref, a_ref, apow_ref, d_ref, x0_ref, g_ref, xl_ref, bu_ref, carry_ref,
             *, seq, rows_per_seq_tile):
    tc = u_ref.shape[0]
    ns = S5_GB * S5_STATE
    long_seq = seq >= tc
    u = u_ref[...]
    bu_ref[...] = jnp.dot(u.astype(BF16), wb_ref[0], preferred_element_type=F32)
    a_re, a_im = a_ref[0, 0:1, :], a_ref[0, 1:2, :]

    def cmul_add(xr, xi, pr, pi, sr, si):
        return xr + pr * sr - pi * si, xi + pr * si + pi * sr

    if long_seq:
        t_idx = pl.program_id(1) % (seq // tc)

        @pl.when(t_idx == 0)
        def _():
            carry_ref[...] = x0_ref[0]

        row = lax.broadcasted_iota(jnp.int32, (8, ns), 0)
        pw = [(apow_ref[0, 0, k - 1:k, :], apow_ref[0, 1, k - 1:k, :]) for k in (1, 2, 4)]
        ap_re, ap_im = apow_ref[0, 0], apow_ref[0, 1]

        def tile_step(j, carry):
            cr, ci = carry
            r0 = pl.multiple_of(j * 8, 8)
            xr = bu_ref[pl.ds(r0, 8), 0:ns]
            xi = bu_ref[pl.ds(r0, 8), ns:2 * ns]
            for k, (pr, pi) in zip((1, 2, 4), pw):
                keep = row >= k
                sr = jnp.where(keep, pltpu.roll(xr, k, 0), 0.0)
                si = jnp.where(keep, pltpu.roll(xi, k, 0), 0.0)
                xr, xi = cmul_add(xr, xi, pr, pi, sr, si)
            xr, xi = cmul_add(xr, xi, ap_re, ap_im, cr, ci)
            bu_ref[pl.ds(r0, 8), 0:ns] = xr
            bu_ref[pl.ds(r0, 8), ns:2 * ns] = xi
            return xr[7:8, :], xi[7:8, :]

        cr, ci = lax.fori_loop(0, tc // 8, tile_step, (carry_ref[0:1, :], carry_ref[1:2, :]))
        carry_ref[0:1, :] = cr
        carry_ref[1:2, :] = ci
        xl_ref[0, 0:1, :] = cr
        xl_ref[0, 1:2, :] = ci
    else:
        xr = bu_ref[:, 0:ns]
        xi = bu_ref[:, ns:2 * ns]
        pos = lax.broadcasted_iota(jnp.int32, (tc, ns), 0) % seq
        k = 1
        pr, pi = a_re, a_im
        while k < seq:
            keep = pos >= k
            sr = jnp.where(keep, pltpu.roll(xr, k, 0), 0.0)
            si = jnp.where(keep, pltpu.roll(xi, k, 0), 0.0)
            xr, xi = cmul_add(xr, xi, pr, pi, sr, si)
            pr, pi = pr * pr - pi * pi, 2.0 * pr * pi
            k *= 2
        ap_re = jnp.concatenate([apow_ref[0, 0, 0:seq, :]] * (tc // seq), axis=0)
        ap_im = jnp.concatenate([apow_ref[0, 1, 0:seq, :]] * (tc // seq), axis=0)
        xr, xi = cmul_add(xr, xi, ap_re, ap_im, x0_ref[:, 0:ns], x0_ref[:, ns:2 * ns])
        bu_ref[:, 0:ns] = xr
        bu_ref[:, ns:2 * ns] = xi
        xl_ref[...] = bu_ref[...]

    y = jnp.dot(bu_ref[...].astype(BF16), wc_ref[0], preferred_element_type=F32) + d_ref[...] * u
    g_ref[...] = _gelu_tanh(y)


def _s5_discretize(lam_re, lam_im, log_dt, b_re, b_im, c_re, c_im):
    g, n = lam_re.shape
    dt = jnp.exp(log_dt)[:, None]
    mag = jnp.exp(lam_re * dt)
    ab_re, ab_im = mag * jnp.cos(lam_im * dt), mag * jnp.sin(lam_im * dt)
    nr, ni = ab_re - 1.0, ab_im
    den = lam_re * lam_re + lam_im * lam_im
    fr, fi = (nr * lam_re + ni * lam_im) / den, (ni * lam_re - nr * lam_im) / den
    bb_re = fr[..., None] * b_re - fi[..., None] * b_im
    bb_im = fr[..., None] * b_im + fi[..., None] * b_re
    nblk = g // S5_GB
    eye = jnp.eye(S5_GB, dtype=F32)

    def expand_b(bb):
        bb = bb.reshape(nblk, S5_GB, n, S5_GROUP)
        return jnp.einsum("bgnj,gh->bgjhn", bb, eye).reshape(nblk, S5_GB * S5_GROUP, S5_GB * n)

    def expand_c(cc):
        cc = cc.reshape(nblk, S5_GB, S5_GROUP, n)
        return jnp.einsum("bgjn,gh->bgnhj", cc, eye).reshape(nblk, S5_GB * n, S5_GB * S5_GROUP)

    wb = jnp.concatenate([expand_b(bb_re), expand_b(bb_im)], axis=2).astype(BF16)
    wc = jnp.concatenate([expand_c(c_re), -expand_c(c_im)], axis=1).astype(BF16)
    kk = jnp.arange(1, 9, dtype=F32)[:, None, None]
    magk = jnp.exp(kk * lam_re * dt)
    pk_re, pk_im = magk * jnp.cos(kk * lam_im * dt), magk * jnp.sin(kk * lam_im * dt)
    apow = jnp.stack([pk_re, pk_im], axis=0).reshape(2, 8, nblk, S5_GB * n).transpose(2, 0, 1, 3)
    a1 = jnp.stack([ab_re, ab_im], axis=0).reshape(2, nblk, S5_GB * n).transpose(1, 0, 2)
    return wb, wc, a1, apow


def _s5_mix(u_src, col0, x0_re, x0_im, prm, seq):
    wb, wc, a1, apow, d_skip = prm
    m = u_src.shape[0]
    nblk = wb.shape[0]
    cb = S5_GB * S5_GROUP
    ns = S5_GB * S5_STATE
    width = nblk * cb
    nseq = m // seq
    tc = _pick_tm(m) if seq >= 8 else m
    long_seq = seq >= tc
    n_tiles = m // tc
    cblk0 = col0 // cb
    x0 = jnp.concatenate([x0_re.reshape(nseq, nblk, ns), x0_im.reshape(nseq, nblk, ns)], axis=2)
    if long_seq:
        tiles_per_seq = seq // tc
        x0_arr = x0.transpose(1, 0, 2).reshape(nblk * nseq, 2, ns)
        x0_spec = pl.BlockSpec((1, 2, ns), lambda gb, r: (gb * nseq + r // tiles_per_seq, 0, 0))
        xl_shape = jax.ShapeDtypeStruct((nblk * nseq, 2, ns), F32)
        xl_spec = pl.BlockSpec((1, 2, ns), lambda gb, r: (gb * nseq + r // tiles_per_seq, 0, 0))
        carry_shape = (2, ns)
    else:
        x0_rows = jnp.repeat(x0, seq, axis=0)
        x0_arr = x0_rows.reshape(m, nblk * 2 * ns)
        x0_spec = pl.BlockSpec((tc, 2 * ns), lambda gb, r: (r, gb))
        xl_shape = jax.ShapeDtypeStruct((m, nblk * 2 * ns), F32)
        xl_spec = pl.BlockSpec((tc, 2 * ns), lambda gb, r: (r, gb))
        carry_shape = (2, ns)
    body = functools.partial(_s5_body, seq=seq, rows_per_seq_tile=tc)
    g, xl = pl.pallas_call(
        body,
        grid=(nblk, n_tiles),
        in_specs=[
            pl.BlockSpec((tc, cb), lambda gb, r: (r, cblk0 + gb)),
            pl.BlockSpec((1, cb, 2 * ns), lambda gb, r: (gb, 0, 0)),
            pl.BlockSpec((1, 2 * ns, cb), lambda gb, r: (gb, 0, 0)),
            pl.BlockSpec((1, 2, ns), lambda gb, r: (gb, 0, 0)),
            pl.BlockSpec((1, 2, 8, ns), lambda gb, r: (gb, 0, 0, 0)),
            pl.BlockSpec((1, cb), lambda gb, r: (0, gb)),
            x0_spec,
        ],
        out_specs=[pl.BlockSpec((tc, cb), lambda gb, r: (r, gb)), xl_spec],
        out_shape=[jax.ShapeDtypeStruct((m, width), F32), xl_shape],
        scratch_shapes=[pltpu.VMEM((tc, 2 * ns), F32), pltpu.VMEM(carry_shape, F32)],
        compiler_params=_cparams(("parallel", "arbitrary")),
        name="s5_mix",
    )(u_src, wb, wc, a1, apow, d_skip, x0_arr)
    if long_seq:
        xl = xl.reshape(nblk, nseq, 2, ns).transpose(2, 1, 0, 3).reshape(2, nseq, nblk * ns)
    else:
        xl = xl.reshape(nseq, seq, nblk, 2, ns)[:, seq - 1].transpose(2, 0, 1, 3).reshape(2, nseq, nblk * ns)
    return g, xl[0], xl[1]


def _s5_layer(x, g_norm, x0_re, x0_im, w_in, mix_prm, w_glu, b_glu, w_o):
    b, t, d = x.shape
    m = b * t
    x2 = x.reshape(m, d)
    width = w_glu.shape[0]
    z = _mm_simple("s5_in", _pro_rms, [x2, g_norm], ["row", "vec"], w_in)
    g, xr, xi = _s5_mix(z, 0, x0_re, x0_im, mix_prm, t)
    g2 = _mm_simple("s5_glu", _pro_cast, [g], ["row"], w_glu, _epi_glu, [g, b_glu], ["tile", "nvec"])
    tm = _pick_tm(m)
    gate_spec = pl.BlockSpec((tm, width), lambda i, j: (i, 1))
    y = _mm("s5_out", _pro_gate, [g2, z], [_row_spec(tm, width), gate_spec], w_o, _epi_res, [x2],
            [_tile_spec(tm, 512)], [jax.ShapeDtypeStruct((m, d), F32)], [_tile_spec(tm, 512)], tm, 512)[0]
    ng = width // S5_GROUP
    return y.reshape(b, t, d), xr.reshape(b, ng, S5_STATE), xi.reshape(b, ng, S5_STATE)


def _rwkv_body(r_ref, k_ref, v_ref, dw_ref, da_ref, w0_ref, a0_ref, kk_ref, ka_ref, rk_ref, gw_ref, gb_ref,
               s0_ref, y_ref, sout_ref, state_ref, *, t_valid, n_chunks, pairs):
    lc = r_ref.shape[1]
    l2 = 2 * lc
    ci = pl.program_id(2)
    prs = range(pairs)

    def tile(x, p):
        return x[:, p * LANE:(p + 1) * LANE]

    @pl.when(ci == 0)
    def _():
        zero = jnp.zeros((RW_HEAD, RW_HEAD), F32)
        for p in prs:
            top = jnp.concatenate([s0_ref[0, 2 * p], zero], axis=1)
            bot = jnp.concatenate([zero, s0_ref[0, 2 * p + 1]], axis=1)
            state_ref[p] = jnp.concatenate([top, bot], axis=0)

    lane = lax.broadcasted_iota(jnp.int32, (lc, LANE), 1)
    lo = lane < RW_HEAD
    row = lax.broadcasted_iota(jnp.int32, (l2, l2), 0)
    col = lax.broadcasted_iota(jnp.int32, (l2, l2), 1)
    same = (row // lc) == (col // lc)
    strict = same & ((col % lc) < (row % lc))
    incl = same & ((col % lc) <= (row % lc))
    eye = (col == row).astype(F32)
    r1 = lax.broadcasted_iota(jnp.int32, (lc, lc), 0)
    c1 = lax.broadcasted_iota(jnp.int32, (lc, lc), 1)
    tri_incl = (c1 <= r1).astype(F32)

    def seg_sum(x):
        se = jnp.sum(jnp.where(lo, x, 0.0), axis=-1, keepdims=True)
        so = jnp.sum(jnp.where(lo, 0.0, x), axis=-1, keepdims=True)
        return jnp.where(lo, se, so)

    def stack(x):
        return jnp.concatenate([jnp.where(lo, x, 0.0), jnp.where(lo, 0.0, x)], axis=0)

    r = r_ref[0]
    k = k_ref[0]
    v = v_ref[0]
    wl = -_softplus(-(w0_ref[...] + dw_ref[0])) - 0.5
    lw = -jnp.exp(wl)
    a_rate = _sigmoid(a0_ref[...] + da_ref[0])
    kkr = k * kk_ref[...]
    kf = k * (1.0 + (a_rate - 1.0) * ka_ref[...])
    rkf = r * kf * rk_ref[...]
    if t_valid < lc:
        valid = lax.broadcasted_iota(jnp.int32, lw.shape, 0) < t_valid
        lw = jnp.where(valid, lw, 0.0)
        kkr = jnp.where(valid, kkr, 0.0)
        kf = jnp.where(valid, kf, 0.0)
    cl = jnp.dot(tri_incl, lw, preferred_element_type=F32, precision=lax.Precision.HIGHEST)
    p_in = jnp.exp(cl)
    p_ex = jnp.exp(cl - lw)
    p_inv = jnp.exp(-cl)
    p_end = jnp.exp(cl[lc - 1:lc, :] - cl)
    rt_full = r * p_in

    kk = [tile(kkr, p) for p in prs]
    kk = [x / jnp.maximum(jnp.sqrt(seg_sum(x * x)), 1e-12) for x in kk]
    bv = [kk[p] * tile(a_rate, p) for p in prs]
    at_st = [stack(-kk[p] * tile(p_ex, p)).astype(BF16) for p in prs]
    rt_st = [stack(tile(rt_full, p)).astype(BF16) for p in prs]
    bt_st = [stack(bv[p] * tile(p_inv, p)).astype(BF16) for p in prs]
    kt_st = [stack(tile(kf, p) * tile(p_inv, p)).astype(BF16) for p in prs]
    v_st = [stack(tile(v, p)).astype(BF16) for p in prs]
    bp_st = [stack(bv[p] * tile(p_end, p)).astype(BF16) for p in prs]
    kp_st = [stack(tile(kf, p) * tile(p_end, p)).astype(BF16) for p in prs]

    def nt(a, b):
        return lax.dot_general(a, b, _NT, preferred_element_type=F32)

    def nn(a, b):
        return jnp.dot(a.astype(BF16), b.astype(BF16), preferred_element_type=F32)

    if l2 % LANE == 0:
        amat = [nt(jnp.concatenate([at_st[p], rt_st[p]], axis=0), jnp.concatenate([bt_st[p], kt_st[p]], axis=0))
                for p in prs]
        n_ab = [jnp.where(strict, m[:l2, :l2], 0.0) for m in amat]
        a_ak = [jnp.where(strict, m[:l2, l2:], 0.0) for m in amat]
        a_rb = [jnp.where(incl, m[l2:, :l2], 0.0) for m in amat]
        a_rk = [jnp.where(incl, m[l2:, l2:], 0.0) for m in amat]
    else:
        n_ab = [jnp.where(strict, nt(at_st[p], bt_st[p]), 0.0) for p in prs]
        a_ak = [jnp.where(strict, nt(at_st[p], kt_st[p]), 0.0) for p in prs]
        a_rb = [jnp.where(incl, nt(rt_st[p], bt_st[p]), 0.0) for p in prs]
        a_rk = [jnp.where(incl, nt(rt_st[p], kt_st[p]), 0.0) for p in prs]

    tinv = [eye + n for n in n_ab]
    npow = n_ab
    span = 2
    while span < min(lc, t_valid):
        npow = [nn(n, n) for n in npow]
        tinv = [t + nn(n, t) for n, t in zip(npow, tinv)]
        span *= 2

    s_old = [state_ref[p] for p in prs]
    s_bf = [x.astype(BF16) for x in s_old]
    rhs_u = [nt(at_st[p], s_bf[p]) + nn(a_ak[p], v_st[p]) for p in prs]
    u_bf = [nn(tinv[p], rhs_u[p]).astype(BF16) for p in prs]
    y_st = [nt(rt_st[p], s_bf[p]) + nn(a_rb[p], u_bf[p]) + nn(a_rk[p], v_st[p]) for p in prs]
    for p in prs:
        s_new = s_old[p] * tile(p_in, p)[lc - 1:lc, :]
        s_new = s_new + lax.dot_general(u_bf[p], bp_st[p], _TN, preferred_element_type=F32)
        s_new = s_new + lax.dot_general(v_st[p], kp_st[p], _TN, preferred_element_type=F32)
        state_ref[p] = s_new

    inv_n = 1.0 / RW_HEAD
    outs = []
    for p in prs:
        y = y_st[p][:lc] + y_st[p][lc:]
        yc = y - seg_sum(y) * inv_n
        var = seg_sum(yc * yc) * inv_n
        yn = yc * lax.rsqrt(var + GN_EPS) * tile(gw_ref[...], p) + tile(gb_ref[...], p)
        outs.append(yn + seg_sum(tile(rkf, p)) * tile(v, p))
    y_ref[0] = jnp.concatenate(outs, axis=-1)

    @pl.when(ci == n_chunks - 1)
    def _():
        for p in prs:
            sbd = state_ref[p]
            sout_ref[0, 2 * p] = sbd[:RW_HEAD, :RW_HEAD]
            sout_ref[0, 2 * p + 1] = sbd[RW_HEAD:, RW_HEAD:]


def _rwkv_core(r, k, v, dw, da, vecs, s0, t_valid):
    b, t, w = r.shape
    lc = min(RW_CHUNK, t)
    n_chunks = t // lc
    pairs = min(w // LANE, 16 if lc <= 8 else 8)
    wb = pairs * LANE
    nhb = w // wb
    hpb = 2 * pairs
    seq_spec = pl.BlockSpec((1, lc, wb), lambda bi, hi, ci: (bi, ci, hi))
    vec_spec = pl.BlockSpec((1, wb), lambda bi, hi, ci: (0, hi))
    st_spec = pl.BlockSpec((1, hpb, RW_HEAD, RW_HEAD), lambda bi, hi, ci: (bi, hi, 0, 0))
    body = functools.partial(_rwkv_body, t_valid=t_valid, n_chunks=n_chunks, pairs=pairs)
    return pl.pallas_call(
        body,
        grid=(b, nhb, n_chunks),
        in_specs=[seq_spec] * 5 + [vec_spec] * 7 + [st_spec],
        out_specs=[seq_spec, st_spec],
        out_shape=[jax.ShapeDtypeStruct((b, t, w), F32), jax.ShapeDtypeStruct(s0.shape, F32)],
        scratch_shapes=[pltpu.VMEM((pairs, LANE, LANE), F32)],
        compiler_params=_cparams(("parallel", "parallel", "arbitrary")),
        name="rwkv7_core",
    )(r, k, v, dw, da, *vecs, s0)


def _rwkv_layer(x, g_norm, shift0, s0, wts):
    b, t, d = x.shape
    m = b * t
    x2 = x.reshape(m, d)
    h = _rms_rows(x2, g_norm)
    h3 = h.reshape(b, t, d)
    prev = jnp.concatenate([shift0[:, None, :], h3[:, :-1]], axis=1).reshape(m, d)
    mu = wts["mu"]

    def mixed(name, n, w):
        return _mm_simple(name, _pro_mix, [h, prev, mu[n:n + 1]], ["row", "row", "vec"], w)

    r = mixed("rwkv_r", 0, wts["w_r"])
    k = mixed("rwkv_k", 2, wts["w_k"])
    v = mixed("rwkv_v", 3, wts["w_v"])
    gpre = mixed("rwkv_g", 5, wts["w_g"])
    dw = _mm_simple("rwkv_w2", _pro_tanh, [mixed("rwkv_w1", 1, wts["w1"])], ["row"], wts["w2"])
    da = _mm_simple("rwkv_a2", _pro_cast, [mixed("rwkv_a1", 4, wts["a1"])], ["row"], wts["a2"])
    tp = t if t % 8 == 0 else 8 * ((t + 7) // 8)

    def seq3(a):
        a = a.reshape(b, t, -1)
        return a if tp == t else jnp.pad(a, ((0, 0), (0, tp - t), (0, 0)))

    y, s_fin = _rwkv_core(seq3(r), seq3(k), seq3(v), seq3(dw), seq3(da), wts["vecs"], s0, t)
    y = y[:, :t].reshape(m, -1)
    out = _mm_simple("rwkv_out", _pro_gate, [y, gpre], ["row", "row"], wts["w_o"], _epi_res, [x2], ["tile"])
    return out.reshape(b, t, d), h3[:, -1], s_fin


def _rwkv_weights(mu, w_r, w_k, w_v, w_g, w_o, w0, w1, w2, a0, a1, a2, k_k, k_a, r_k, gn_w, gn_b):
    lora = w1.shape[1]
    lpad = LANE * ((lora + LANE - 1) // LANE)

    def pad_cols(w):
        return jnp.pad(w, ((0, 0), (0, lpad - lora))).astype(BF16)

    def pad_rows(w):
        return jnp.pad(w, ((0, lpad - lora), (0, 0))).astype(BF16)

    row = lambda a: a.reshape(1, -1)
    return dict(
        mu=mu, w_r=w_r.astype(BF16), w_k=w_k.astype(BF16), w_v=w_v.astype(BF16), w_g=w_g.astype(BF16),
        w_o=w_o.astype(BF16), w1=pad_cols(w1), w2=pad_rows(w2), a1=pad_cols(a1), a2=pad_rows(a2),
        vecs=[row(w0), row(a0), row(k_k), row(k_a), row(r_k), row(gn_w), row(gn_b)],
    )


def kernel(x_prompt, x_sample, cache_mla, page_table, state_s5_re, state_s5_im, state_rwkv, state_rwkv_shift, norm_g, mla_w_in, mla_g_qn, mla_g_qr, mla_g_kv, mla_g_kr, mla_g_kn, mla_w_uk, mla_w_uv, mla_w_o, s5_w_in, s5_lam_re, s5_lam_im, s5_log_dt, s5_b_re, s5_b_im, s5_c_re, s5_c_im, s5_d, s5_w_glu, s5_b_glu, s5_w_o, rwkv_mu, rwkv_w_r, rwkv_w_k, rwkv_w_v, rwkv_w_g, rwkv_w_o, rwkv_w0, rwkv_w1, rwkv_w2, rwkv_a0, rwkv_a1, rwkv_a2, rwkv_k_k, rwkv_k_a, rwkv_r_k, rwkv_gn_w, rwkv_gn_b):
    depth = norm_g.shape[0]
    nb, seq, d = x_prompt.shape
    db, t_dec, _ = x_sample.shape
    past = page_table.shape[1] * PAGE
    c4p, s4p = _rope_tables(jnp.arange(seq, dtype=jnp.int32))
    c4s, s4s = _rope_tables(past + jnp.arange(t_dec, dtype=jnp.int32))
    c4s, s4s = jnp.tile(c4s, (db, 1)), jnp.tile(s4s, (db, 1))
    yp, ys = x_prompt, x_sample
    mla_p, mla_s = [], []
    s5r_p, s5i_p, s5r_s, s5i_s = [], [], [], []
    rw_p, rw_s, sh_p, sh_s = [], [], [], []
    for i in range(depth):
        kind, j = i % 3, i // 3
        g_norm = norm_g[i].reshape(1, d)
        if kind == 0:
            wts = _mla_weights(mla_w_in[j], mla_g_qn[j], mla_g_qr[j], mla_g_kv[j], mla_g_kr[j], mla_g_kn[j],
                               mla_w_uk[j], mla_w_uv[j], mla_w_o[j])
            yp, lat = _mla_prompt(yp, g_norm, wts, c4p, s4p)
            mla_p.append(lat)
            ys, lat = _mla_sample(ys, g_norm, wts, c4s, s4s, cache_mla, page_table, j)
            mla_s.append(lat)
        elif kind == 1:
            wb, wc, a1, apow = _s5_discretize(s5_lam_re[j], s5_lam_im[j], s5_log_dt[j], s5_b_re[j], s5_b_im[j],
                                              s5_c_re[j], s5_c_im[j])
            prm = (wb, wc, a1, apow, s5_d[j].reshape(1, -1))
            args = (s5_w_in[j].astype(BF16), prm, s5_w_glu[j].astype(BF16), s5_b_glu[j].reshape(1, -1),
                    s5_w_o[j].astype(BF16))
            ng, ns = s5_lam_re.shape[1], s5_lam_re.shape[2]
            zero = jnp.zeros((nb, ng, ns), F32)
            yp, xr, xi = _s5_layer(yp, g_norm, zero, zero, *args)
            s5r_p.append(xr)
            s5i_p.append(xi)
            ys, xr, xi = _s5_layer(ys, g_norm, state_s5_re[j], state_s5_im[j], *args)
            s5r_s.append(xr)
            s5i_s.append(xi)
        else:
            wts = _rwkv_weights(rwkv_mu[j], rwkv_w_r[j], rwkv_w_k[j], rwkv_w_v[j], rwkv_w_g[j], rwkv_w_o[j],
                                rwkv_w0[j], rwkv_w1[j], rwkv_w2[j], rwkv_a0[j], rwkv_a1[j], rwkv_a2[j],
                                rwkv_k_k[j], rwkv_k_a[j], rwkv_r_k[j].reshape(-1), rwkv_gn_w[j], rwkv_gn_b[j])
            nh = rwkv_r_k.shape[1]
            sh0 = jnp.zeros((nb, d), F32)
            st0 = jnp.zeros((nb, nh, RW_HEAD, RW_HEAD), F32)
            yp, sh, st = _rwkv_layer(yp, g_norm, sh0, st0, wts)
            sh_p.append(sh)
            rw_p.append(st)
            ys, sh, st = _rwkv_layer(ys, g_norm, state_rwkv_shift[j], state_rwkv[j], wts)
            sh_s.append(sh)
            rw_s.append(st)
    return (yp, ys, jnp.stack(mla_p), jnp.stack(mla_s),
            jnp.stack(s5r_p), jnp.stack(s5i_p), jnp.stack(s5r_s), jnp.stack(s5i_s),
            jnp.stack(rw_p), jnp.stack(rw_s), jnp.stack(sh_p), jnp.stack(sh_s))
```

```python
import functools
import math

import jax
import jax.numpy as jnp
from jax import lax
from jax.experimental import pallas as pl
from jax.experimental.pallas import tpu as pltpu

F32 = jnp.float32
BF16 = jnp.bfloat16

EPS = 1e-6
PAGE = 128
HEADS = 16
D_NOPE = 128
D_ROPE = 64
D_QK = D_NOPE + D_ROPE
KV_LORA = 512
LATENT = KV_LORA + D_ROPE
ROPE_THETA = 10000.0
SM_SCALE = D_QK ** -0.5
NEG = -1e30
S5_GROUP = 16
S5_STATE = 64
S5_GB = 16
RW_HEAD = 64
RW_CHUNK = 64
GN_EPS = 64e-5
DEC_GROUP = 16
LANE = 128
VMEM_LIMIT = 48 * 1024 * 1024
MM_VMEM_BUDGET = 36 * 1024 * 1024

_NT = (((1,), (1,)), ((), ()))
_TN = (((0,), (0,)), ((), ()))


def _cparams(sem):
    return pltpu.CompilerParams(dimension_semantics=sem, vmem_limit_bytes=VMEM_LIMIT)


def _sigmoid(x):
    return 1.0 / (1.0 + jnp.exp(-x))


def _silu(x):
    return x * _sigmoid(x)


def _gelu_tanh(x):
    return 0.5 * x * (1.0 + jnp.tanh(0.7978845608028654 * (x + 0.044715 * x * x * x)))


def _softplus(x):
    return jnp.maximum(x, 0.0) + jnp.log(1.0 + jnp.exp(-jnp.abs(x)))


def _mm_body(*refs, pro, n_pro, epi, n_epi, n_out):
    pro_refs = refs[:n_pro]
    w_ref = refs[n_pro]
    epi_refs = refs[n_pro + 1:n_pro + 1 + n_epi]
    out_refs = refs[n_pro + 1 + n_epi:n_pro + 1 + n_epi + n_out]
    xs_ref = refs[-1]

    @pl.when(pl.program_id(1) == 0)
    def _():
        xs_ref[...] = pro(*[r[...] for r in pro_refs]).astype(BF16)

    acc = jnp.dot(xs_ref[...], w_ref[pl.program_id(1)], preferred_element_type=F32)
    epi(acc, epi_refs, out_refs)


def _mm(name, pro, pro_args, pro_specs, w, epi, epi_args, epi_specs, out_shapes, out_specs, tm, tn):
    m = pro_args[0].shape[0]
    k, n = w.shape
    assert m % tm == 0 and n % tn == 0, (name, m, tm, n, tn)
    body = functools.partial(_mm_body, pro=pro, n_pro=len(pro_args), epi=epi, n_epi=len(epi_args),
                             n_out=len(out_shapes))
    w = w.reshape(k, n // tn, tn).transpose(1, 0, 2)
    w_spec = pl.BlockSpec((n // tn, k, tn), lambda i, j: (0, 0, 0), pipeline_mode=pl.Buffered(1))
    return pl.pallas_call(
        body,
        grid=(m // tm, n // tn),
        in_specs=[*pro_specs, w_spec, *epi_specs],
        out_specs=out_specs,
        out_shape=out_shapes,
        scratch_shapes=[pltpu.VMEM((tm, k), BF16)],
        compiler_params=_cparams(("parallel", "arbitrary")),
        name=name,
    )(*pro_args, w, *epi_args)


def _row_spec(tm, k):
    return pl.BlockSpec((tm, k), lambda i, j: (i, 0))


def _vec_spec(k):
    return pl.BlockSpec((1, k), lambda i, j: (0, 0))


def _tile_spec(tm, tn):
    return pl.BlockSpec((tm, tn), lambda i, j: (i, j))


def _nvec_spec(tn):
    return pl.BlockSpec((1, tn), lambda i, j: (0, j))


def _pro_rms(x, g):
    return x * lax.rsqrt(jnp.mean(x * x, axis=-1, keepdims=True) + EPS) * g


def _pro_gate(a, b):
    return a * _silu(b)


def _pro_mix(h, prev, mu):
    return h + (prev - h) * mu


def _pro_cast(x):
    return x


def _pro_tanh(x):
    return jnp.tanh(x)


def _epi_plain(acc, epi_refs, out_refs):
    out_refs[0][...] = acc.astype(out_refs[0].dtype)


def _epi_res(acc, epi_refs, out_refs):
    out_refs[0][...] = epi_refs[0][...] + acc


def _epi_glu(acc, epi_refs, out_refs):
    g_ref, b_ref = epi_refs
    out_refs[0][...] = g_ref[...] * _sigmoid(acc + b_ref[...])


def _pick_tm(m):
    return 512 if m % 512 == 0 else m


def _pick_tn(n):
    for t in (2048, 1024, 512, 256, 128):
        if n % t == 0:
            return t
    return n


def _mm_vmem_bytes(tm, tn, k, n, n_row_in, n_tile_epi, out_itemsize):
    return (2 * 4 * tm * k * n_row_in + 2 * k * n + 2 * 4 * tm * tn * n_tile_epi + 2 * out_itemsize * tm * tn
            + 2 * tm * k + 2 * 4 * tm * tn)


def _mm_simple(name, pro, pro_args, pro_kinds, w, epi=_epi_plain, epi_args=(), epi_kinds=(),
               out_dtype=F32):
    m = pro_args[0].shape[0]
    k, n = w.shape
    tn = _pick_tn(n)
    tm = _pick_tm(m)
    while tm > 128 and m % (tm // 2) == 0 and _mm_vmem_bytes(
            tm, tn, k, n, pro_kinds.count("row"), epi_kinds.count("tile"), jnp.dtype(out_dtype).itemsize) > MM_VMEM_BUDGET:
        tm //= 2
    pro_specs = [_row_spec(tm, k) if kd == "row" else _vec_spec(k) for kd in pro_kinds]
    epi_specs = [_tile_spec(tm, tn) if kd == "tile" else _nvec_spec(tn) for kd in epi_kinds]
    return _mm(name, pro, list(pro_args), pro_specs, w, epi, list(epi_args), epi_specs,
               [jax.ShapeDtypeStruct((m, n), out_dtype)], [_tile_spec(tm, tn)], tm, tn)[0]


def _rms_body(x_ref, g_ref, o_ref):
    o_ref[...] = _pro_rms(x_ref[...], g_ref[...])


def _rms_rows(x, g):
    m, k = x.shape
    tm = _pick_tm(m)
    return pl.pallas_call(
        _rms_body,
        grid=(m // tm,),
        in_specs=[pl.BlockSpec((tm, k), lambda i: (i, 0)), pl.BlockSpec((1, k), lambda i: (0, 0))],
        out_specs=pl.BlockSpec((tm, k), lambda i: (i, 0)),
        out_shape=jax.ShapeDtypeStruct((m, k), F32),
        compiler_params=_cparams(("parallel",)),
        name="rms_rows",
    )(x, g)


def _epi_qnope(acc, epi_refs, out_refs, *, four_d):
    g_ref, = epi_refs
    o_ref, = out_refs
    for hh in range(acc.shape[1] // LANE):
        a = acc[:, hh * LANE:(hh + 1) * LANE]
        y = a * lax.rsqrt(jnp.mean(a * a, axis=-1, keepdims=True) + EPS) * g_ref[...] * SM_SCALE
        if four_d:
            o_ref[0, hh] = y.astype(o_ref.dtype)
        else:
            o_ref[:, hh * LANE:(hh + 1) * LANE] = y.astype(o_ref.dtype)


def _rope_pair_tile(x, g2, c4, s4):
    lane = lax.broadcasted_iota(jnp.int32, x.shape, 1)
    lo = lane < D_ROPE
    x2 = x * x
    ss_e = jnp.sum(jnp.where(lo, x2, 0.0), axis=-1, keepdims=True)
    ss_o = jnp.sum(jnp.where(lo, 0.0, x2), axis=-1, keepdims=True)
    inv = jnp.where(lo, lax.rsqrt(ss_e * (1.0 / D_ROPE) + EPS), lax.rsqrt(ss_o * (1.0 / D_ROPE) + EPS))
    xn = x * inv * g2
    first_half = (lane % D_ROPE) < (D_ROPE // 2)
    partner = jnp.where(first_half, pltpu.roll(xn, LANE - D_ROPE // 2, 1), pltpu.roll(xn, D_ROPE // 2, 1))
    return (xn * c4 + partner * s4) * SM_SCALE, lo


def _epi_qrope(acc, epi_refs, out_refs, *, four_d):
    g2_ref, c4_ref, s4_ref = epi_refs
    o_ref, = out_refs
    for p in range(acc.shape[1] // LANE):
        r, lo = _rope_pair_tile(acc[:, p * LANE:(p + 1) * LANE], g2_ref[...], c4_ref[...], s4_ref[...])
        if four_d:
            o_ref[0, 2 * p] = jnp.where(lo, r, 0.0).astype(o_ref.dtype)
            o_ref[0, 2 * p + 1] = jnp.where(lo, 0.0, r).astype(o_ref.dtype)
        else:
            o_ref[:, p * LANE:(p + 1) * LANE] = r.astype(o_ref.dtype)


def _epi_latent(acc, epi_refs, out_refs):
    gkv_ref, g2_ref, g2p_ref, c4_ref, s4_ref = epi_refs
    lat_ref, cn_ref, kr_ref = out_refs
    c = acc[:, :KV_LORA]
    cn = c * lax.rsqrt(jnp.mean(c * c, axis=-1, keepdims=True) + EPS) * gkv_ref[...]
    lat_ref[:, :KV_LORA] = cn
    cn_ref[...] = cn.astype(cn_ref.dtype)
    d = acc[:, KV_LORA:KV_LORA + LANE]
    dp = acc[:, KV_LORA + LANE:KV_LORA + 2 * LANE]
    inv = lax.rsqrt(jnp.mean(d * d, axis=-1, keepdims=True) + EPS)
    r = d * inv * g2_ref[...] * c4_ref[...] + dp * inv * g2p_ref[...] * s4_ref[...]
    lat_ref[:, KV_LORA:LATENT] = r[:, :D_ROPE]
    kr_ref[...] = r.astype(kr_ref.dtype)


def _epi_knope(acc, epi_refs, out_refs):
    g_ref, = epi_refs
    o_ref, = out_refs
    for hh in range(acc.shape[1] // LANE):
        a = acc[:, hh * LANE:(hh + 1) * LANE]
        y = a * lax.rsqrt(jnp.mean(a * a, axis=-1, keepdims=True) + EPS) * g_ref[...]
        o_ref[0, hh] = y.astype(o_ref.dtype)


def _epi_heads(acc, epi_refs, out_refs):
    o_ref, = out_refs
    for hh in range(acc.shape[1] // LANE):
        o_ref[0, hh] = acc[:, hh * LANE:(hh + 1) * LANE].astype(o_ref.dtype)


def _flash_body(qt_ref, kt_ref, qn_ref, qr_ref, kn_ref, kr_ref, v_ref, o_ref, m_ref, l_ref, acc_ref, *, hb):
    t = pl.program_id(2)
    qi = qt_ref[t]
    ki = kt_ref[t]
    heads = range(hb)

    @pl.when(ki == 0)
    def _():
        m_ref[...] = jnp.full(m_ref.shape, NEG, F32)
        l_ref[...] = jnp.zeros(l_ref.shape, F32)
        acc_ref[...] = jnp.zeros(acc_ref.shape, F32)

    def step(masked):
        k_rope = kr_ref[0]
        qs = [jnp.concatenate([qn_ref[0, h], qr_ref[0, h]], axis=-1) for h in heads]
        ks = [jnp.concatenate([kn_ref[0, h], k_rope], axis=-1) for h in heads]
        ss = [lax.dot_general(q, k, _NT, preferred_element_type=F32) for q, k in zip(qs, ks)]
        if masked:
            keep = (lax.broadcasted_iota(jnp.int32, ss[0].shape, 1) <= lax.broadcasted_iota(jnp.int32, ss[0].shape, 0))
            ss = [jnp.where(keep, s, NEG) for s in ss]
        m_old = [m_ref[h] for h in heads]
        m_new = [jnp.maximum(mo, jnp.max(s, axis=-1, keepdims=True)) for mo, s in zip(m_old, ss)]
        alpha = [jnp.exp(mo - mn) for mo, mn in zip(m_old, m_new)]
        ps = [jnp.exp(s - mn) for s, mn in zip(ss, m_new)]
        pv = [jnp.dot(p.astype(BF16), v_ref[0, h], preferred_element_type=F32) for h, p in zip(heads, ps)]
        for h in heads:
            l_ref[h] = alpha[h] * l_ref[h] + jnp.sum(ps[h], axis=-1, keepdims=True)
            acc_ref[h] = alpha[h] * acc_ref[h] + pv[h]
            m_ref[h] = m_new[h]

    @pl.when(ki < qi)
    def _():
        step(False)

    @pl.when(ki == qi)
    def _():
        step(True)
        o_ref[0] = jnp.concatenate([acc_ref[h] / l_ref[h] for h in heads], axis=-1)


def _flash_attention(qn, qr, kn, kr, v):
    b, h, s, _ = qn.shape
    tq = min(512, s)
    nq = s // tq
    hb = 2
    pairs = [(qi, ki) for qi in range(nq) for ki in range(qi + 1)]
    qtab = jnp.asarray([p[0] for p in pairs], jnp.int32)
    ktab = jnp.asarray([p[1] for p in pairs], jnp.int32)
    q_spec = pl.BlockSpec((1, hb, tq, LANE), lambda bi, hi, t, qt, kt: (bi, hi, qt[t], 0))
    k_spec = pl.BlockSpec((1, hb, tq, LANE), lambda bi, hi, t, qt, kt: (bi, hi, kt[t], 0))
    grid_spec = pltpu.PrefetchScalarGridSpec(
        num_scalar_prefetch=2,
        grid=(b, h // hb, len(pairs)),
        in_specs=[q_spec, q_spec, k_spec,
                  pl.BlockSpec((1, tq, LANE), lambda bi, hi, t, qt, kt: (bi, kt[t], 0)), k_spec],
        out_specs=pl.BlockSpec((1, tq, hb * LANE), lambda bi, hi, t, qt, kt: (bi, qt[t], hi)),
        scratch_shapes=[pltpu.VMEM((hb, tq, 1), F32), pltpu.VMEM((hb, tq, 1), F32),
                        pltpu.VMEM((hb, tq, LANE), F32)],
    )
    return pl.pallas_call(
        functools.partial(_flash_body, hb=hb),
        grid_spec=grid_spec,
        out_shape=jax.ShapeDtypeStruct((b, s, h * LANE), F32),
        compiler_params=_cparams(("parallel", "parallel", "arbitrary")),
        name="mla_prompt_attn",
    )(qtab, ktab, qn, qr, kn, kr, v)


def _dec_attn_body(pt_ref, *refs, pps, group, n_steps, t_dec):
    cache_refs = refs[:pps]
    wuk_ref, qp_ref, qr_ref, cnew_ref, krnew_ref, wuv_ref = refs[pps:pps + 6]
    o_ref = refs[pps + 6]
    wq_ref, m_ref, l_ref, acc_ref = refs[pps + 7:]
    del pt_ref
    i = pl.program_id(1)
    n_up = HEADS * D_NOPE
    rows = t_dec * HEADS

    @pl.when(i == 0)
    def _():
        wq_ref[0:n_up, :] = wuk_ref[...]
        wq_ref[n_up:n_up + rows, :] = qp_ref[0]
        m_ref[...] = jnp.full(m_ref.shape, NEG, F32)
        l_ref[...] = jnp.zeros(l_ref.shape, F32)
        acc_ref[...] = jnp.zeros(acc_ref.shape, F32)

    def up_project(ct_bf):
        return jnp.dot(wq_ref[...], ct_bf, preferred_element_type=F32)

    def process(res, ct_bf, krt_bf, mask):
        n = ct_bf.shape[1]
        kn = res[:n_up].reshape(HEADS, D_NOPE, n)
        inv = lax.rsqrt(jnp.sum(kn * kn, axis=1) * (1.0 / D_NOPE) + EPS)
        s = res[n_up:] * jnp.concatenate([inv] * t_dec, axis=0)
        s = s + jnp.dot(qr_ref[0], krt_bf, preferred_element_type=F32)
        if mask is not None:
            s = jnp.where(mask, s, NEG)
        m_old = m_ref[...]
        m_new = jnp.maximum(m_old, jnp.max(s, axis=-1, keepdims=True))
        alpha = jnp.exp(m_old - m_new)
        p = jnp.exp(s - m_new)
        l_ref[...] = alpha * l_ref[...] + jnp.sum(p, axis=-1, keepdims=True)
        pc = lax.dot_general(p.astype(BF16), ct_bf, _NT, preferred_element_type=F32)
        acc_ref[...] = alpha * acc_ref[...] + pc
        m_ref[...] = m_new

    for u in range(0, pps, group):
        lats = [cache_refs[u + g][0, 0] for g in range(group)]
        ct_bf = jnp.concatenate([lt[:KV_LORA].astype(BF16) for lt in lats], axis=1)
        krt_bf = jnp.concatenate([lt[KV_LORA:LATENT].astype(BF16) for lt in lats], axis=1)
        process(up_project(ct_bf), ct_bf, krt_bf, None)

    @pl.when(i == n_steps - 1)
    def _():
        npad = cnew_ref.shape[2]
        key = lax.broadcasted_iota(jnp.int32, (rows, npad), 1)
        tq = lax.broadcasted_iota(jnp.int32, (rows, npad), 0) // HEADS
        process(up_project(cnew_ref[0]), cnew_ref[0], krnew_ref[0], key <= tq)
        o_lat = (acc_ref[...] / l_ref[...]).astype(BF16)
        full = jnp.dot(o_lat, wuv_ref[...], preferred_element_type=F32)
        rh = lax.broadcasted_iota(jnp.int32, full.shape, 0) % HEADS
        ch = lax.broadcasted_iota(jnp.int32, full.shape, 1) // LANE
        diag = jnp.where(rh == ch, full, 0.0).reshape(t_dec, HEADS, full.shape[1])
        o_ref[0] = jnp.sum(diag, axis=1)


def _dec_attention(cache, page_table, wuk_t, qp, qr, cnew, krnew, wuv, layer):
    bdec, n_pages = page_table.shape
    rows = qp.shape[1]
    t_dec = rows // HEADS
    pps = 16 if n_pages % 16 == 0 else 2
    group = min(pps, DEC_GROUP)
    n_steps = n_pages // pps
    n_up = HEADS * D_NOPE

    def cache_spec(u):
        return pl.BlockSpec((1, 1, LATENT, PAGE), lambda b, i, pt: (layer, pt[b, i * pps + u], 0, 0))

    body = functools.partial(_dec_attn_body, pps=pps, group=group, n_steps=n_steps, t_dec=t_dec)
    grid_spec = pltpu.PrefetchScalarGridSpec(
        num_scalar_prefetch=1,
        grid=(bdec, n_steps),
        in_specs=[
            *[cache_spec(u) for u in range(pps)],
            pl.BlockSpec((n_up, KV_LORA), lambda b, i, pt: (0, 0)),
            pl.BlockSpec((1, rows, KV_LORA), lambda b, i, pt: (b, 0, 0)),
            pl.BlockSpec((1, rows, D_ROPE), lambda b, i, pt: (b, 0, 0)),
            pl.BlockSpec((1, KV_LORA, PAGE), lambda b, i, pt: (b, 0, 0)),
            pl.BlockSpec((1, D_ROPE, PAGE), lambda b, i, pt: (b, 0, 0)),
            pl.BlockSpec((KV_LORA, n_up), lambda b, i, pt: (0, 0)),
        ],
        out_specs=pl.BlockSpec((1, t_dec, n_up), lambda b, i, pt: (b, 0, 0)),
        scratch_shapes=[
            pltpu.VMEM((n_up + rows, KV_LORA), BF16),
            pltpu.VMEM((rows, 1), F32),
            pltpu.VMEM((rows, 1), F32),
            pltpu.VMEM((rows, KV_LORA), F32),
        ],
    )
    return pl.pallas_call(
        body,
        grid_spec=grid_spec,
        out_shape=jax.ShapeDtypeStruct((bdec, t_dec, n_up), F32),
        compiler_params=_cparams(("parallel", "arbitrary")),
        name="mla_sample_attn",
    )(page_table, *([cache] * pps), wuk_t, qp, qr, cnew, krnew, wuv)


def _absorb_body(qn_ref, g_ref, wuk_ref, o_ref):
    q = (qn_ref[...] * g_ref[...]).astype(BF16)
    o_ref[0] = lax.dot_general(q, wuk_ref[...], _NT, preferred_element_type=F32).astype(o_ref.dtype)


def _absorb_q(qn, g_kn, wuk):
    m = qn.shape[0]
    return pl.pallas_call(
        _absorb_body,
        grid=(HEADS,),
        in_specs=[
            pl.BlockSpec((m, LANE), lambda h: (0, h)),
            pl.BlockSpec((1, LANE), lambda h: (0, 0)),
            pl.BlockSpec((KV_LORA, LANE), lambda h: (0, h)),
        ],
        out_specs=pl.BlockSpec((1, m, KV_LORA), lambda h: (h, 0, 0)),
        out_shape=jax.ShapeDtypeStruct((HEADS, m, KV_LORA), BF16),
        compiler_params=_cparams(("parallel",)),
        name="mla_absorb_q",
    )(qn, g_kn, wuk)


def _rope_tables(pos):
    inv = ROPE_THETA ** (-jnp.arange(0, D_ROPE, 2, dtype=F32) / D_ROPE)
    ang = pos.astype(F32)[:, None] * inv[None, :]
    c, s = jnp.cos(ang), jnp.sin(ang)
    return jnp.tile(c, (1, 4)), jnp.concatenate([-s, s, -s, s], axis=1)


def _mla_weights(w_in, g_qn, g_qr, g_kv, g_kr, g_kn, w_uk, w_uv, w_o):
    d = w_in.shape[0]
    q_dim = HEADS * D_QK
    wq = w_in[:, :q_dim].reshape(d, HEADS, D_QK)
    w_kr = w_in[:, q_dim + KV_LORA:q_dim + LATENT]
    w_kr_sw = jnp.concatenate([w_kr[:, D_ROPE // 2:], w_kr[:, :D_ROPE // 2]], axis=1)
    g_kr_sw = jnp.concatenate([g_kr[D_ROPE // 2:], g_kr[:D_ROPE // 2]])
    wuk2 = w_uk.reshape(KV_LORA, HEADS * D_NOPE)
    return dict(
        w_qn=wq[:, :, :D_NOPE].reshape(d, HEADS * D_NOPE).astype(BF16),
        w_qr=wq[:, :, D_NOPE:].reshape(d, HEADS * D_ROPE).astype(BF16),
        w_lat=jnp.concatenate([w_in[:, q_dim:q_dim + KV_LORA], w_kr, w_kr, w_kr_sw, w_kr_sw], axis=1).astype(BF16),
        w_gate=w_in[:, q_dim + LATENT:].astype(BF16),
        g_qn=g_qn.reshape(1, D_NOPE),
        g_qr2=jnp.tile(g_qr, 2).reshape(1, LANE),
        g_kv=g_kv.reshape(1, KV_LORA),
        g_kr2=jnp.tile(g_kr, 2).reshape(1, LANE),
        g_kr2p=jnp.tile(g_kr_sw, 2).reshape(1, LANE),
        g_kn=g_kn.reshape(1, D_NOPE),
        wuk=wuk2.astype(BF16),
        wuk_t=wuk2.T.astype(BF16),
        wuv=w_uv.reshape(KV_LORA, HEADS * D_NOPE).astype(BF16),
        w_o=w_o.astype(BF16),
    )


def _mla_project(x2, g_norm, wts, c4, s4, seq_4d):
    m, d = x2.shape
    tm = _pick_tm(m)
    four_d = seq_4d is not None
    n_tab = c4.shape[0] // tm
    tab_spec = pl.BlockSpec((tm, LANE), lambda i, j: (i % n_tab, 0))
    lane_vec = pl.BlockSpec((1, LANE), lambda i, j: (0, 0))
    pro_args, pro_specs = [x2, g_norm], [_row_spec(tm, d), _vec_spec(d)]

    def q_out(n_heads_blk, n_cols):
        if four_d:
            nsb = seq_4d // tm
            shape = jax.ShapeDtypeStruct((m // seq_4d, HEADS, seq_4d, LANE), BF16)
            spec = pl.BlockSpec((1, n_heads_blk, tm, LANE), lambda i, j: (i // nsb, j, i % nsb, 0))
        else:
            shape = jax.ShapeDtypeStruct((m, n_cols), F32)
            spec = _tile_spec(tm, n_cols)
        return [shape], [spec]

    shp, spc = q_out(HEADS, HEADS * D_NOPE)
    qn = _mm("mla_qnope", _pro_rms, pro_args, pro_specs, wts["w_qn"],
             functools.partial(_epi_qnope, four_d=four_d), [wts["g_qn"]], [lane_vec], shp, spc, tm,
             HEADS * D_NOPE)[0]
    shp, spc = q_out(HEADS, HEADS * D_ROPE)
    qr = _mm("mla_qrope", _pro_rms, pro_args, pro_specs, wts["w_qr"],
             functools.partial(_epi_qrope, four_d=four_d), [wts["g_qr2"], c4, s4],
             [lane_vec, tab_spec, tab_spec], shp, spc, tm, HEADS * D_ROPE)[0]
    n_lat = wts["w_lat"].shape[1]
    lat, cn, kr = _mm(
        "mla_latent", _pro_rms, pro_args, pro_specs, wts["w_lat"], _epi_latent,
        [wts["g_kv"], wts["g_kr2"], wts["g_kr2p"], c4, s4],
        [pl.BlockSpec((1, KV_LORA), lambda i, j: (0, 0)), lane_vec, lane_vec, tab_spec, tab_spec],
        [jax.ShapeDtypeStruct((m, LATENT), F32), jax.ShapeDtypeStruct((m, KV_LORA), BF16),
         jax.ShapeDtypeStruct((m, LANE), BF16)],
        [pl.BlockSpec((tm, LATENT), lambda i, j: (i, 0)), pl.BlockSpec((tm, KV_LORA), lambda i, j: (i, 0)),
         pl.BlockSpec((tm, LANE), lambda i, j: (i, 0))],
        tm, n_lat)
    gate = _mm_simple("mla_gate", _pro_rms, pro_args, ["row", "vec"], wts["w_gate"])
    return qn, qr, lat, cn, kr, gate


def _mla_prompt(x, g_norm, wts, c4, s4):
    b, s, d = x.shape
    x2 = x.reshape(b * s, d)
    tm = _pick_tm(b * s)
    nsb = s // tm
    qn, qr, lat, cn, kr, gate = _mla_project(x2, g_norm, wts, c4, s4, s)
    head_shape = [jax.ShapeDtypeStruct((b, HEADS, s, LANE), BF16)]
    head_spec = [pl.BlockSpec((1, HEADS, tm, LANE), lambda i, j: (i // nsb, j, i % nsb, 0))]
    kn = _mm("mla_knope", _pro_cast, [cn], [_row_spec(tm, KV_LORA)], wts["wuk"], _epi_knope, [wts["g_kn"]],
             [pl.BlockSpec((1, LANE), lambda i, j: (0, 0))], head_shape, head_spec, tm, HEADS * D_NOPE)[0]
    v = _mm("mla_value", _pro_cast, [cn], [_row_spec(tm, KV_LORA)], wts["wuv"], _epi_heads, [], [],
            head_shape, head_spec, tm, HEADS * D_NOPE)[0]
    o = _flash_attention(qn, qr, kn, kr.reshape(b, s, LANE), v).reshape(b * s, HEADS * D_NOPE)
    y = _mm_simple("mla_out", _pro_gate, [o, gate], ["row", "row"], wts["w_o"], _epi_res, [x2], ["tile"])
    return y.reshape(b, s, d), lat.reshape(b, s, LATENT)


def _mla_sample(x, g_norm, wts, c4, s4, cache, page_table, layer):
    b, t, d = x.shape
    m = b * t
    x2 = x.reshape(m, d)
    qn, qr, lat, cn, kr, gate = _mla_project(x2, g_norm, wts, c4, s4, None)
    qp = _absorb_q(qn, wts["g_kn"], wts["wuk"])
    qp = qp.reshape(HEADS, b, t, KV_LORA).transpose(1, 2, 0, 3).reshape(b, t * HEADS, KV_LORA)
    qr = qr.reshape(b, t * HEADS, D_ROPE).astype(BF16)
    cnew = jnp.pad(cn.reshape(b, t, KV_LORA).transpose(0, 2, 1), ((0, 0), (0, 0), (0, PAGE - t)))
    krnew = jnp.pad(kr[:, :D_ROPE].reshape(b, t, D_ROPE).transpose(0, 2, 1), ((0, 0), (0, 0), (0, PAGE - t)))
    o = _dec_attention(cache, page_table, wts["wuk_t"], qp, qr, cnew, krnew, wts["wuv"], layer)
    o = o.reshape(m, HEADS * D_NOPE)
    y = _mm_simple("mla_out_s", _pro_gate, [o, gate], ["row", "row"], wts["w_o"], _epi_res, [x2], ["tile"])
    return y.reshape(b, t, d), lat.reshape(b, t, LATENT)


def _s5_body(u_ref, wb_ref, wc_ref, a_ref, apow_ref, d_ref, x0_ref, g_ref, xl_ref, bu_ref, carry_ref,
             *, seq, rows_per_seq_tile):
    tc = u_ref.shape[0]
    ns = S5_GB * S5_STATE
    long_seq = seq >= tc
    u = u_ref[...]
    bu_ref[...] = jnp.dot(u.astype(BF16), wb_ref[0], preferred_element_type=F32)
    a_re, a_im = a_ref[0, 0:1, :], a_ref[0, 1:2, :]

    def cmul_add(xr, xi, pr, pi, sr, si):
        return xr + pr * sr - pi * si, xi + pr * si + pi * sr

    if long_seq:
        t_idx = pl.program_id(1) % (seq // tc)

        @pl.when(t_idx == 0)
        def _():
            carry_ref[...] = x0_ref[0]

        row = lax.broadcasted_iota(jnp.int32, (8, ns), 0)
        pw = [(apow_ref[0, 0, k - 1:k, :], apow_ref[0, 1, k - 1:k, :]) for k in (1, 2, 4)]
        ap_re, ap_im = apow_ref[0, 0], apow_ref[0, 1]

        def tile_step(j, carry):
            cr, ci = carry
            r0 = pl.multiple_of(j * 8, 8)
            xr = bu_ref[pl.ds(r0, 8), 0:ns]
            xi = bu_ref[pl.ds(r0, 8), ns:2 * ns]
            for k, (pr, pi) in zip((1, 2, 4), pw):
                keep = row >= k
                sr = jnp.where(keep, pltpu.roll(xr, k, 0), 0.0)
                si = jnp.where(keep, pltpu.roll(xi, k, 0), 0.0)
                xr, xi = cmul_add(xr, xi, pr, pi, sr, si)
            xr, xi = cmul_add(xr, xi, ap_re, ap_im, cr, ci)
            bu_ref[pl.ds(r0, 8), 0:ns] = xr
            bu_ref[pl.ds(r0, 8), ns:2 * ns] = xi
            return xr[7:8, :], xi[7:8, :]

        cr, ci = lax.fori_loop(0, tc // 8, tile_step, (carry_ref[0:1, :], carry_ref[1:2, :]))
        carry_ref[0:1, :] = cr
        carry_ref[1:2, :] = ci
        xl_ref[0, 0:1, :] = cr
        xl_ref[0, 1:2, :] = ci
    else:
        xr = bu_ref[:, 0:ns]
        xi = bu_ref[:, ns:2 * ns]
        pos = lax.broadcasted_iota(jnp.int32, (tc, ns), 0) % seq
        k = 1
        pr, pi = a_re, a_im
        while k < seq:
            keep = pos >= k
            sr = jnp.where(keep, pltpu.roll(xr, k, 0), 0.0)
            si = jnp.where(keep, pltpu.roll(xi, k, 0), 0.0)
            xr, xi = cmul_add(xr, xi, pr, pi, sr, si)
            pr, pi = pr * pr - pi * pi, 2.0 * pr * pi
            k *= 2
        ap_re = jnp.concatenate([apow_ref[0, 0, 0:seq, :]] * (tc // seq), axis=0)
        ap_im = jnp.concatenate([apow_ref[0, 1, 0:seq, :]] * (tc // seq), axis=0)
        xr, xi = cmul_add(xr, xi, ap_re, ap_im, x0_ref[:, 0:ns], x0_ref[:, ns:2 * ns])
        bu_ref[:, 0:ns] = xr
        bu_ref[:, ns:2 * ns] = xi
        xl_ref[...] = bu_ref[...]

    y = jnp.dot(bu_ref[...].astype(BF16), wc_ref[0], preferred_element_type=F32) + d_ref[...] * u
    g_ref[...] = _gelu_tanh(y)


def _s5_discretize(lam_re, lam_im, log_dt, b_re, b_im, c_re, c_im):
    g, n = lam_re.shape
    dt = jnp.exp(log_dt)[:, None]
    mag = jnp.exp(lam_re * dt)
    ab_re, ab_im = mag * jnp.cos(lam_im * dt), mag * jnp.sin(lam_im * dt)
    nr, ni = ab_re - 1.0, ab_im
    den = lam_re * lam_re + lam_im * lam_im
    fr, fi = (nr * lam_re + ni * lam_im) / den, (ni * lam_re - nr * lam_im) / den
    bb_re = fr[..., None] * b_re - fi[..., None] * b_im
    bb_im = fr[..., None] * b_im + fi[..., None] * b_re
    nblk = g // S5_GB
    eye = jnp.eye(S5_GB, dtype=F32)

    def expand_b(bb):
        bb = bb.reshape(nblk, S5_GB, n, S5_GROUP)
        return jnp.einsum("bgnj,gh->bgjhn", bb, eye).reshape(nblk, S5_GB * S5_GROUP, S5_GB * n)

    def expand_c(cc):
        cc = cc.reshape(nblk, S5_GB, S5_GROUP, n)
        return jnp.einsum("bgjn,gh->bgnhj", cc, eye).reshape(nblk, S5_GB * n, S5_GB * S5_GROUP)

    wb = jnp.concatenate([expand_b(bb_re), expand_b(bb_im)], axis=2).astype(BF16)
    wc = jnp.concatenate([expand_c(c_re), -expand_c(c_im)], axis=1).astype(BF16)
    kk = jnp.arange(1, 9, dtype=F32)[:, None, None]
    magk = jnp.exp(kk * lam_re * dt)
    pk_re, pk_im = magk * jnp.cos(kk * lam_im * dt), magk * jnp.sin(kk * lam_im * dt)
    apow = jnp.stack([pk_re, pk_im], axis=0).reshape(2, 8, nblk, S5_GB * n).transpose(2, 0, 1, 3)
    a1 = jnp.stack([ab_re, ab_im], axis=0).reshape(2, nblk, S5_GB * n).transpose(1, 0, 2)
    return wb, wc, a1, apow


def _s5_mix(u_src, col0, x0_re, x0_im, prm, seq):
    wb, wc, a1, apow, d_skip = prm
    m = u_src.shape[0]
    nblk = wb.shape[0]
    cb = S5_GB * S5_GROUP
    ns = S5_GB * S5_STATE
    width = nblk * cb
    nseq = m // seq
    tc = _pick_tm(m) if seq >= 8 else m
    long_seq = seq >= tc
    n_tiles = m // tc
    cblk0 = col0 // cb
    x0 = jnp.concatenate([x0_re.reshape(nseq, nblk, ns), x0_im.reshape(nseq, nblk, ns)], axis=2)
    if long_seq:
        tiles_per_seq = seq // tc
        x0_arr = x0.transpose(1, 0, 2).reshape(nblk * nseq, 2, ns)
        x0_spec = pl.BlockSpec((1, 2, ns), lambda gb, r: (gb * nseq + r // tiles_per_seq, 0, 0))
        xl_shape = jax.ShapeDtypeStruct((nblk * nseq, 2, ns), F32)
        xl_spec = pl.BlockSpec((1, 2, ns), lambda gb, r: (gb * nseq + r // tiles_per_seq, 0, 0))
        carry_shape = (2, ns)
    else:
        x0_rows = jnp.repeat(x0, seq, axis=0)
        x0_arr = x0_rows.reshape(m, nblk * 2 * ns)
        x0_spec = pl.BlockSpec((tc, 2 * ns), lambda gb, r: (r, gb))
        xl_shape = jax.ShapeDtypeStruct((m, nblk * 2 * ns), F32)
        xl_spec = pl.BlockSpec((tc, 2 * ns), lambda gb, r: (r, gb))
        carry_shape = (2, ns)
    body = functools.partial(_s5_body, seq=seq, rows_per_seq_tile=tc)
    g, xl = pl.pallas_call(
        body,
        grid=(nblk, n_tiles),
        in_specs=[
            pl.BlockSpec((tc, cb), lambda gb, r: (r, cblk0 + gb)),
            pl.BlockSpec((1, cb, 2 * ns), lambda gb, r: (gb, 0, 0)),
            pl.BlockSpec((1, 2 * ns, cb), lambda gb, r: (gb, 0, 0)),
            pl.BlockSpec((1, 2, ns), lambda gb, r: (gb, 0, 0)),
            pl.BlockSpec((1, 2, 8, ns), lambda gb, r: (gb, 0, 0, 0)),
            pl.BlockSpec((1, cb), lambda gb, r: (0, gb)),
            x0_spec,
        ],
        out_specs=[pl.BlockSpec((tc, cb), lambda gb, r: (r, gb)), xl_spec],
        out_shape=[jax.ShapeDtypeStruct((m, width), F32), xl_shape],
        scratch_shapes=[pltpu.VMEM((tc, 2 * ns), F32), pltpu.VMEM(carry_shape, F32)],
        compiler_params=_cparams(("parallel", "arbitrary")),
        name="s5_mix",
    )(u_src, wb, wc, a1, apow, d_skip, x0_arr)
    if long_seq:
        xl = xl.reshape(nblk, nseq, 2, ns).transpose(2, 1, 0, 3).reshape(2, nseq, nblk * ns)
    else:
        xl = xl.reshape(nseq, seq, nblk, 2, ns)[:, seq - 1].transpose(2, 0, 1, 3).reshape(2, nseq, nblk * ns)
    return g, xl[0], xl[1]


def _s5_layer(x, g_norm, x0_re, x0_im, w_in, mix_prm, w_glu, b_glu, w_o):
    b, t, d = x.shape
    m = b * t
    x2 = x.reshape(m, d)
    width = w_glu.shape[0]
    z = _mm_simple("s5_in", _pro_rms, [x2, g_norm], ["row", "vec"], w_in)
    g, xr, xi = _s5_mix(z, 0, x0_re, x0_im, mix_prm, t)
    g2 = _mm_simple("s5_glu", _pro_cast, [g], ["row"], w_glu, _epi_glu, [g, b_glu], ["tile", "nvec"])
    tm = 256 if m % 256 == 0 else m
    gate_spec = pl.BlockSpec((tm, width), lambda i, j: (i, 1))
    y = _mm("s5_out", _pro_gate, [g2, z], [_row_spec(tm, width), gate_spec], w_o, _epi_res, [x2],
            [_tile_spec(tm, d)], [jax.ShapeDtypeStruct((m, d), F32)], [_tile_spec(tm, d)], tm, d)[0]
    ng = width // S5_GROUP
    return y.reshape(b, t, d), xr.reshape(b, ng, S5_STATE), xi.reshape(b, ng, S5_STATE)


def _rwkv_body(r_ref, k_ref, v_ref, dw_ref, da_ref, w0_ref, a0_ref, kk_ref, ka_ref, rk_ref, gw_ref, gb_ref,
               s0_ref, y_ref, sout_ref, state_ref, *, t_valid, n_chunks, pairs):
    lc = r_ref.shape[1]
    l2 = 2 * lc
    ci = pl.program_id(2)
    prs = range(pairs)

    def tile(x, p):
        return x[:, p * LANE:(p + 1) * LANE]

    @pl.when(ci == 0)
    def _():
        zero = jnp.zeros((RW_HEAD, RW_HEAD), F32)
        for p in prs:
            top = jnp.concatenate([s0_ref[0, 2 * p], zero], axis=1)
            bot = jnp.concatenate([zero, s0_ref[0, 2 * p + 1]], axis=1)
            state_ref[p] = jnp.concatenate([top, bot], axis=0)

    lane = lax.broadcasted_iota(jnp.int32, (lc, LANE), 1)
    lo = lane < RW_HEAD
    row = lax.broadcasted_iota(jnp.int32, (l2, l2), 0)
    col = lax.broadcasted_iota(jnp.int32, (l2, l2), 1)
    same = (row // lc) == (col // lc)
    strict = same & ((col % lc) < (row % lc))
    incl = same & ((col % lc) <= (row % lc))
    eye = (col == row).astype(F32)
    r1 = lax.broadcasted_iota(jnp.int32, (lc, lc), 0)
    c1 = lax.broadcasted_iota(jnp.int32, (lc, lc), 1)
    tri_incl = (c1 <= r1).astype(F32)

    def seg_sum(x):
        se = jnp.sum(jnp.where(lo, x, 0.0), axis=-1, keepdims=True)
        so = jnp.sum(jnp.where(lo, 0.0, x), axis=-1, keepdims=True)
        return jnp.where(lo, se, so)

    def stack(x):
        return jnp.concatenate([jnp.where(lo, x, 0.0), jnp.where(lo, 0.0, x)], axis=0)

    r = r_ref[0]
    k = k_ref[0]
    v = v_ref[0]
    wl = -_softplus(-(w0_ref[...] + dw_ref[0])) - 0.5
    lw = -jnp.exp(wl)
    a_rate = _sigmoid(a0_ref[...] + da_ref[0])
    kkr = k * kk_ref[...]
    kf = k * (1.0 + (a_rate - 1.0) * ka_ref[...])
    rkf = r * kf * rk_ref[...]
    if t_valid < lc:
        valid = lax.broadcasted_iota(jnp.int32, lw.shape, 0) < t_valid
        lw = jnp.where(valid, lw, 0.0)
        kkr = jnp.where(valid, kkr, 0.0)
        kf = jnp.where(valid, kf, 0.0)
    cl = jnp.dot(tri_incl, lw, preferred_element_type=F32, precision=lax.Precision.HIGHEST)
    p_in = jnp.exp(cl)
    p_ex = jnp.exp(cl - lw)
    p_inv = jnp.exp(-cl)
    p_end = jnp.exp(cl[lc - 1:lc, :] - cl)
    rt_full = r * p_in

    kk = [tile(kkr, p) for p in prs]
    kk = [x / jnp.maximum(jnp.sqrt(seg_sum(x * x)), 1e-12) for x in kk]
    bv = [kk[p] * tile(a_rate, p) for p in prs]
    at_st = [stack(-kk[p] * tile(p_ex, p)).astype(BF16) for p in prs]
    rt_st = [stack(tile(rt_full, p)).astype(BF16) for p in prs]
    bt_st = [stack(bv[p] * tile(p_inv, p)).astype(BF16) for p in prs]
    kt_st = [stack(tile(kf, p) * tile(p_inv, p)).astype(BF16) for p in prs]
    v_st = [stack(tile(v, p)).astype(BF16) for p in prs]
    bp_st = [stack(bv[p] * tile(p_end, p)).astype(BF16) for p in prs]
    kp_st = [stack(tile(kf, p) * tile(p_end, p)).astype(BF16) for p in prs]

    def nt(a, b):
        return lax.dot_general(a, b, _NT, preferred_element_type=F32)

    def nn(a, b):
        return jnp.dot(a.astype(BF16), b.astype(BF16), preferred_element_type=F32)

    if l2 % LANE == 0:
        amat = [nt(jnp.concatenate([at_st[p], rt_st[p]], axis=0), jnp.concatenate([bt_st[p], kt_st[p]], axis=0))
                for p in prs]
        n_ab = [jnp.where(strict, m[:l2, :l2], 0.0) for m in amat]
        a_ak = [jnp.where(strict, m[:l2, l2:], 0.0) for m in amat]
        a_rb = [jnp.where(incl, m[l2:, :l2], 0.0) for m in amat]
        a_rk = [jnp.where(incl, m[l2:, l2:], 0.0) for m in amat]
    else:
        n_ab = [jnp.where(strict, nt(at_st[p], bt_st[p]), 0.0) for p in prs]
        a_ak = [jnp.where(strict, nt(at_st[p], kt_st[p]), 0.0) for p in prs]
        a_rb = [jnp.where(incl, nt(rt_st[p], bt_st[p]), 0.0) for p in prs]
        a_rk = [jnp.where(incl, nt(rt_st[p], kt_st[p]), 0.0) for p in prs]

    tinv = [eye + n for n in n_ab]
    npow = n_ab
    span = 2
    while span < min(lc, t_valid):
        npow = [nn(n, n) for n in npow]
        tinv = [t + nn(n, t) for n, t in zip(npow, tinv)]
        span *= 2

    s_old = [state_ref[p] for p in prs]
    s_bf = [x.astype(BF16) for x in s_old]
    rhs_u = [nt(at_st[p], s_bf[p]) + nn(a_ak[p], v_st[p]) for p in prs]
    u_bf = [nn(tinv[p], rhs_u[p]).astype(BF16) for p in prs]
    y_st = [nt(rt_st[p], s_bf[p]) + nn(a_rb[p], u_bf[p]) + nn(a_rk[p], v_st[p]) for p in prs]
    for p in prs:
        s_new = s_old[p] * tile(p_in, p)[lc - 1:lc, :]
        s_new = s_new + lax.dot_general(u_bf[p], bp_st[p], _TN, preferred_element_type=F32)
        s_new = s_new + lax.dot_general(v_st[p], kp_st[p], _TN, preferred_element_type=F32)
        state_ref[p] = s_new

    inv_n = 1.0 / RW_HEAD
    outs = []
    for p in prs:
        y = y_st[p][:lc] + y_st[p][lc:]
        yc = y - seg_sum(y) * inv_n
        var = seg_sum(yc * yc) * inv_n
        yn = yc * lax.rsqrt(var + GN_EPS) * tile(gw_ref[...], p) + tile(gb_ref[...], p)
        outs.append(yn + seg_sum(tile(rkf, p)) * tile(v, p))
    y_ref[0] = jnp.concatenate(outs, axis=-1)

    @pl.when(ci == n_chunks - 1)
    def _():
        for p in prs:
            sbd = state_ref[p]
            sout_ref[0, 2 * p] = sbd[:RW_HEAD, :RW_HEAD]
            sout_ref[0, 2 * p + 1] = sbd[RW_HEAD:, RW_HEAD:]


def _rwkv_core(r, k, v, dw, da, vecs, s0, t_valid):
    b, t, w = r.shape
    lc = min(RW_CHUNK, t)
    n_chunks = t // lc
    pairs = min(w // LANE, 16 if lc <= 8 else 8)
    wb = pairs * LANE
    nhb = w // wb
    hpb = 2 * pairs
    seq_spec = pl.BlockSpec((1, lc, wb), lambda bi, hi, ci: (bi, ci, hi))
    vec_spec = pl.BlockSpec((1, wb), lambda bi, hi, ci: (0, hi))
    st_spec = pl.BlockSpec((1, hpb, RW_HEAD, RW_HEAD), lambda bi, hi, ci: (bi, hi, 0, 0))
    body = functools.partial(_rwkv_body, t_valid=t_valid, n_chunks=n_chunks, pairs=pairs)
    return pl.pallas_call(
        body,
        grid=(b, nhb, n_chunks),
        in_specs=[seq_spec] * 5 + [vec_spec] * 7 + [st_spec],
        out_specs=[seq_spec, st_spec],
        out_shape=[jax.ShapeDtypeStruct((b, t, w), F32), jax.ShapeDtypeStruct(s0.shape, F32)],
        scratch_shapes=[pltpu.VMEM((pairs, LANE, LANE), F32)],
        compiler_params=_cparams(("parallel", "parallel", "arbitrary")),
        name="rwkv7_core",
    )(r, k, v, dw, da, *vecs, s0)


def _rwkv_layer(x, g_norm, shift0, s0, wts):
    b, t, d = x.shape
    m = b * t
    x2 = x.reshape(m, d)
    h = _rms_rows(x2, g_norm)
    h3 = h.reshape(b, t, d)
    prev = jnp.concatenate([shift0[:, None, :], h3[:, :-1]], axis=1).reshape(m, d)
    mu = wts["mu"]

    def mixed(name, n, w):
        return _mm_simple(name, _pro_mix, [h, prev, mu[n:n + 1]], ["row", "row", "vec"], w)

    r = mixed("rwkv_r", 0, wts["w_r"])
    k = mixed("rwkv_k", 2, wts["w_k"])
    v = mixed("rwkv_v", 3, wts["w_v"])
    gpre = mixed("rwkv_g", 5, wts["w_g"])
    dw = _mm_simple("rwkv_w2", _pro_tanh, [mixed("rwkv_w1", 1, wts["w1"])], ["row"], wts["w2"])
    da = _mm_simple("rwkv_a2", _pro_cast, [mixed("rwkv_a1", 4, wts["a1"])], ["row"], wts["a2"])
    tp = t if t % 8 == 0 else 8 * ((t + 7) // 8)

    def seq3(a):
        a = a.reshape(b, t, -1)
        return a if tp == t else jnp.pad(a, ((0, 0), (0, tp - t), (0, 0)))

    y, s_fin = _rwkv_core(seq3(r), seq3(k), seq3(v), seq3(dw), seq3(da), wts["vecs"], s0, t)
    y = y[:, :t].reshape(m, -1)
    out = _mm_simple("rwkv_out", _pro_gate, [y, gpre], ["row", "row"], wts["w_o"], _epi_res, [x2], ["tile"])
    return out.reshape(b, t, d), h3[:, -1], s_fin


def _rwkv_weights(mu, w_r, w_k, w_v, w_g, w_o, w0, w1, w2, a0, a1, a2, k_k, k_a, r_k, gn_w, gn_b):
    lora = w1.shape[1]
    lpad = LANE * ((lora + LANE - 1) // LANE)

    def pad_cols(w):
        return jnp.pad(w, ((0, 0), (0, lpad - lora))).astype(BF16)

    def pad_rows(w):
        return jnp.pad(w, ((0, lpad - lora), (0, 0))).astype(BF16)

    row = lambda a: a.reshape(1, -1)
    return dict(
        mu=mu, w_r=w_r.astype(BF16), w_k=w_k.astype(BF16), w_v=w_v.astype(BF16), w_g=w_g.astype(BF16),
        w_o=w_o.astype(BF16), w1=pad_cols(w1), w2=pad_rows(w2), a1=pad_cols(a1), a2=pad_rows(a2),
        vecs=[row(w0), row(a0), row(k_k), row(k_a), row(r_k), row(gn_w), row(gn_b)],
    )


def kernel(x_prompt, x_sample, cache_mla, page_table, state_s5_re, state_s5_im, state_rwkv, state_rwkv_shift, norm_g, mla_w_in, mla_g_qn, mla_g_qr, mla_g_kv, mla_g_kr, mla_g_kn, mla_w_uk, mla_w_uv, mla_w_o, s5_w_in, s5_lam_re, s5_lam_im, s5_log_dt, s5_b_re, s5_b_im, s5_c_re, s5_c_im, s5_d, s5_w_glu, s5_b_glu, s5_w_o, rwkv_mu, rwkv_w_r, rwkv_w_k, rwkv_w_v, rwkv_w_g, rwkv_w_o, rwkv_w0, rwkv_w1, rwkv_w2, rwkv_a0, rwkv_a1, rwkv_a2, rwkv_k_k, rwkv_k_a, rwkv_r_k, rwkv_gn_w, rwkv_gn_b):
    depth = norm_g.shape[0]
    nb, seq, d = x_prompt.shape
    db, t_dec, _ = x_sample.shape
    past = page_table.shape[1] * PAGE
    c4p, s4p = _rope_tables(jnp.arange(seq, dtype=jnp.int32))
    c4s, s4s = _rope_tables(past + jnp.arange(t_dec, dtype=jnp.int32))
    c4s, s4s = jnp.tile(c4s, (db, 1)), jnp.tile(s4s, (db, 1))
    cache_t = jnp.swapaxes(cache_mla, 2, 3)
    yp, ys = x_prompt, x_sample
    mla_p, mla_s = [], []
    s5r_p, s5i_p, s5r_s, s5i_s = [], [], [], []
    rw_p, rw_s, sh_p, sh_s = [], [], [], []
    for i in range(depth):
        kind, j = i % 3, i // 3
        g_norm = norm_g[i].reshape(1, d)
        if kind == 0:
            wts = _mla_weights(mla_w_in[j], mla_g_qn[j], mla_g_qr[j], mla_g_kv[j], mla_g_kr[j], mla_g_kn[j],
                               mla_w_uk[j], mla_w_uv[j], mla_w_o[j])
            yp, lat = _mla_prompt(yp, g_norm, wts, c4p, s4p)
            mla_p.append(lat)
            ys, lat = _mla_sample(ys, g_norm, wts, c4s, s4s, cache_t, page_table, j)
            mla_s.append(lat)
        elif kind == 1:
            wb, wc, a1, apow = _s5_discretize(s5_lam_re[j], s5_lam_im[j], s5_log_dt[j], s5_b_re[j], s5_b_im[j],
                                              s5_c_re[j], s5_c_im[j])
            prm = (wb, wc, a1, apow, s5_d[j].reshape(1, -1))
            args = (s5_w_in[j].astype(BF16), prm, s5_w_glu[j].astype(BF16), s5_b_glu[j].reshape(1, -1),
                    s5_w_o[j].astype(BF16))
            ng, ns = s5_lam_re.shape[1], s5_lam_re.shape[2]
            zero = jnp.zeros((nb, ng, ns), F32)
            yp, xr, xi = _s5_layer(yp, g_norm, zero, zero, *args)
            s5r_p.append(xr)
            s5i_p.append(xi)
            ys, xr, xi = _s5_layer(ys, g_norm, state_s5_re[j], state_s5_im[j], *args)
            s5r_s.append(xr)
            s5i_s.append(xi)
        else:
            wts = _rwkv_weights(rwkv_mu[j], rwkv_w_r[j], rwkv_w_k[j], rwkv_w_v[j], rwkv_w_g[j], rwkv_w_o[j],
                                rwkv_w0[j], rwkv_w1[j], rwkv_w2[j], rwkv_a0[j], rwkv_a1[j], rwkv_a2[j],
                                rwkv_k_k[j], rwkv_k_a[j], rwkv_r_k[j].reshape(-1), rwkv_gn_w[j], rwkv_gn_b[j])
            nh = rwkv_r_k.shape[1]
            sh0 = jnp.zeros((nb, d), F32)
            st0 = jnp.zeros((nb, nh, RW_HEAD, RW_HEAD), F32)
            yp, sh, st = _rwkv_layer(yp, g_norm, sh0, st0, wts)
            sh_p.append(sh)
            rw_p.append(st)
            ys, sh, st = _rwkv_layer(ys, g_norm, state_rwkv_shift[j], state_rwkv[j], wts)
            sh_s.append(sh)
            rw_s.append(st)
    return (yp, ys, jnp.stack(mla_p), jnp.stack(mla_s),
            jnp.stack(s5r_p), jnp.stack(s5i_p), jnp.stack(s5r_s), jnp.stack(s5i_s),
            jnp.stack(rw_p), jnp.stack(rw_s), jnp.stack(sh_p), jnp.stack(sh_s))
```

```python
import functools
import math

import jax
import jax.numpy as jnp
from jax import lax
from jax.experimental import pallas as pl
from jax.experimental.pallas import tpu as pltpu

F32 = jnp.float32
BF16 = jnp.bfloat16

EPS = 1e-6
PAGE = 128
HEADS = 16
D_NOPE = 128
D_ROPE = 64
D_QK = D_NOPE + D_ROPE
KV_LORA = 512
LATENT = KV_LORA + D_ROPE
ROPE_THETA = 10000.0
SM_SCALE = D_QK ** -0.5
NEG = -1e30
S5_GROUP = 16
S5_STATE = 64
S5_GB = 16
RW_HEAD = 64
RW_CHUNK = 64
GN_EPS = 64e-5
DEC_GROUP = 16
LANE = 128
VMEM_LIMIT = 48 * 1024 * 1024
MM_VMEM_BUDGET = 36 * 1024 * 1024

_NT = (((1,), (1,)), ((), ()))
_TN = (((0,), (0,)), ((), ()))


def _cparams(sem):
    return pltpu.CompilerParams(dimension_semantics=sem, vmem_limit_bytes=VMEM_LIMIT)


def _sigmoid(x):
    return 1.0 / (1.0 + jnp.exp(-x))


def _silu(x):
    return x * _sigmoid(x)


def _gelu_tanh(x):
    return 0.5 * x * (1.0 + jnp.tanh(0.7978845608028654 * (x + 0.044715 * x * x * x)))


def _softplus(x):
    return jnp.maximum(x, 0.0) + jnp.log(1.0 + jnp.exp(-jnp.abs(x)))


def _mm_body(*refs, pro, n_pro, epi, n_epi, n_out):
    pro_refs = refs[:n_pro]
    w_ref = refs[n_pro]
    epi_refs = refs[n_pro + 1:n_pro + 1 + n_epi]
    out_refs = refs[n_pro + 1 + n_epi:n_pro + 1 + n_epi + n_out]
    xs_ref = refs[-1]

    @pl.when(pl.program_id(1) == 0)
    def _():
        xs_ref[...] = pro(*[r[...] for r in pro_refs]).astype(BF16)

    acc = jnp.dot(xs_ref[...], w_ref[pl.program_id(1)], preferred_element_type=F32)
    epi(acc, epi_refs, out_refs)


def _mm(name, pro, pro_args, pro_specs, w, epi, epi_args, epi_specs, out_shapes, out_specs, tm, tn):
    m = pro_args[0].shape[0]
    k, n = w.shape
    assert m % tm == 0 and n % tn == 0, (name, m, tm, n, tn)
    body = functools.partial(_mm_body, pro=pro, n_pro=len(pro_args), epi=epi, n_epi=len(epi_args),
                             n_out=len(out_shapes))
    w = w.reshape(k, n // tn, tn).transpose(1, 0, 2)
    w_spec = pl.BlockSpec((n // tn, k, tn), lambda i, j: (0, 0, 0), pipeline_mode=pl.Buffered(1))
    return pl.pallas_call(
        body,
        grid=(m // tm, n // tn),
        in_specs=[*pro_specs, w_spec, *epi_specs],
        out_specs=out_specs,
        out_shape=out_shapes,
        scratch_shapes=[pltpu.VMEM((tm, k), BF16)],
        compiler_params=_cparams(("parallel", "arbitrary")),
        name=name,
    )(*pro_args, w, *epi_args)


def _row_spec(tm, k):
    return pl.BlockSpec((tm, k), lambda i, j: (i, 0))


def _vec_spec(k):
    return pl.BlockSpec((1, k), lambda i, j: (0, 0))


def _tile_spec(tm, tn):
    return pl.BlockSpec((tm, tn), lambda i, j: (i, j))


def _nvec_spec(tn):
    return pl.BlockSpec((1, tn), lambda i, j: (0, j))


def _pro_rms(x, g):
    return x * lax.rsqrt(jnp.mean(x * x, axis=-1, keepdims=True) + EPS) * g


def _pro_gate(a, b):
    return a * _silu(b)


def _pro_mix(h, prev, mu):
    return h + (prev - h) * mu


def _pro_cast(x):
    return x


def _pro_tanh(x):
    return jnp.tanh(x)


def _epi_plain(acc, epi_refs, out_refs):
    out_refs[0][...] = acc.astype(out_refs[0].dtype)


def _epi_res(acc, epi_refs, out_refs):
    out_refs[0][...] = epi_refs[0][...] + acc


def _epi_glu(acc, epi_refs, out_refs):
    g_ref, b_ref = epi_refs
    out_refs[0][...] = g_ref[...] * _sigmoid(acc + b_ref[...])


def _pick_tm(m):
    return 512 if m % 512 == 0 else m


def _pick_tn(n):
    for t in (2048, 1024, 512, 256, 128):
        if n % t == 0:
            return t
    return n


def _mm_vmem_bytes(tm, tn, k, n, n_row_in, n_tile_epi, out_itemsize):
    return (2 * 4 * tm * k * n_row_in + 2 * k * n + 2 * 4 * tm * tn * n_tile_epi + 2 * out_itemsize * tm * tn
            + 2 * tm * k + 2 * 4 * tm * tn)


def _mm_simple(name, pro, pro_args, pro_kinds, w, epi=_epi_plain, epi_args=(), epi_kinds=(),
               out_dtype=F32):
    m = pro_args[0].shape[0]
    k, n = w.shape
    tn = _pick_tn(n)
    tm = _pick_tm(m)
    while tm > 128 and m % (tm // 2) == 0 and _mm_vmem_bytes(
            tm, tn, k, n, pro_kinds.count("row"), epi_kinds.count("tile"), jnp.dtype(out_dtype).itemsize) > MM_VMEM_BUDGET:
        tm //= 2
    pro_specs = [_row_spec(tm, k) if kd == "row" else _vec_spec(k) for kd in pro_kinds]
    epi_specs = [_tile_spec(tm, tn) if kd == "tile" else _nvec_spec(tn) for kd in epi_kinds]
    return _mm(name, pro, list(pro_args), pro_specs, w, epi, list(epi_args), epi_specs,
               [jax.ShapeDtypeStruct((m, n), out_dtype)], [_tile_spec(tm, tn)], tm, tn)[0]


def _rms_body(x_ref, g_ref, o_ref):
    o_ref[...] = _pro_rms(x_ref[...], g_ref[...])


def _rms_rows(x, g):
    m, k = x.shape
    tm = _pick_tm(m)
    return pl.pallas_call(
        _rms_body,
        grid=(m // tm,),
        in_specs=[pl.BlockSpec((tm, k), lambda i: (i, 0)), pl.BlockSpec((1, k), lambda i: (0, 0))],
        out_specs=pl.BlockSpec((tm, k), lambda i: (i, 0)),
        out_shape=jax.ShapeDtypeStruct((m, k), F32),
        compiler_params=_cparams(("parallel",)),
        name="rms_rows",
    )(x, g)


def _epi_qnope(acc, epi_refs, out_refs, *, four_d):
    g_ref, = epi_refs
    o_ref, = out_refs
    for hh in range(acc.shape[1] // LANE):
        a = acc[:, hh * LANE:(hh + 1) * LANE]
        y = a * lax.rsqrt(jnp.mean(a * a, axis=-1, keepdims=True) + EPS) * g_ref[...] * SM_SCALE
        if four_d:
            o_ref[0, hh] = y.astype(o_ref.dtype)
        else:
            o_ref[:, hh * LANE:(hh + 1) * LANE] = y.astype(o_ref.dtype)


def _rope_pair_tile(x, g2, c4, s4):
    lane = lax.broadcasted_iota(jnp.int32, x.shape, 1)
    lo = lane < D_ROPE
    x2 = x * x
    ss_e = jnp.sum(jnp.where(lo, x2, 0.0), axis=-1, keepdims=True)
    ss_o = jnp.sum(jnp.where(lo, 0.0, x2), axis=-1, keepdims=True)
    inv = jnp.where(lo, lax.rsqrt(ss_e * (1.0 / D_ROPE) + EPS), lax.rsqrt(ss_o * (1.0 / D_ROPE) + EPS))
    xn = x * inv * g2
    first_half = (lane % D_ROPE) < (D_ROPE // 2)
    partner = jnp.where(first_half, pltpu.roll(xn, LANE - D_ROPE // 2, 1), pltpu.roll(xn, D_ROPE // 2, 1))
    return (xn * c4 + partner * s4) * SM_SCALE, lo


def _epi_qrope(acc, epi_refs, out_refs, *, four_d):
    g2_ref, c4_ref, s4_ref = epi_refs
    o_ref, = out_refs
    for p in range(acc.shape[1] // LANE):
        r, lo = _rope_pair_tile(acc[:, p * LANE:(p + 1) * LANE], g2_ref[...], c4_ref[...], s4_ref[...])
        if four_d:
            o_ref[0, 2 * p] = jnp.where(lo, r, 0.0).astype(o_ref.dtype)
            o_ref[0, 2 * p + 1] = jnp.where(lo, 0.0, r).astype(o_ref.dtype)
        else:
            o_ref[:, p * LANE:(p + 1) * LANE] = r.astype(o_ref.dtype)


def _epi_latent(acc, epi_refs, out_refs):
    gkv_ref, g2_ref, g2p_ref, c4_ref, s4_ref = epi_refs
    lat_ref, cn_ref, kr_ref = out_refs
    c = acc[:, :KV_LORA]
    cn = c * lax.rsqrt(jnp.mean(c * c, axis=-1, keepdims=True) + EPS) * gkv_ref[...]
    lat_ref[:, :KV_LORA] = cn
    cn_ref[...] = cn.astype(cn_ref.dtype)
    d = acc[:, KV_LORA:KV_LORA + LANE]
    dp = acc[:, KV_LORA + LANE:KV_LORA + 2 * LANE]
    inv = lax.rsqrt(jnp.mean(d * d, axis=-1, keepdims=True) + EPS)
    r = d * inv * g2_ref[...] * c4_ref[...] + dp * inv * g2p_ref[...] * s4_ref[...]
    lat_ref[:, KV_LORA:LATENT] = r[:, :D_ROPE]
    kr_ref[...] = r.astype(kr_ref.dtype)


def _epi_knope(acc, epi_refs, out_refs):
    g_ref, = epi_refs
    o_ref, = out_refs
    for hh in range(acc.shape[1] // LANE):
        a = acc[:, hh * LANE:(hh + 1) * LANE]
        y = a * lax.rsqrt(jnp.mean(a * a, axis=-1, keepdims=True) + EPS) * g_ref[...]
        o_ref[0, hh] = y.astype(o_ref.dtype)


def _epi_heads(acc, epi_refs, out_refs):
    o_ref, = out_refs
    for hh in range(acc.shape[1] // LANE):
        o_ref[0, hh] = acc[:, hh * LANE:(hh + 1) * LANE].astype(o_ref.dtype)


def _flash_body(qt_ref, kt_ref, qn_ref, qr_ref, kn_ref, kr_ref, v_ref, o_ref, m_ref, l_ref, acc_ref, *, hb):
    t = pl.program_id(2)
    qi = qt_ref[t]
    ki = kt_ref[t]
    heads = range(hb)

    @pl.when(ki == 0)
    def _():
        m_ref[...] = jnp.full(m_ref.shape, NEG, F32)
        l_ref[...] = jnp.zeros(l_ref.shape, F32)
        acc_ref[...] = jnp.zeros(acc_ref.shape, F32)

    def step(masked):
        k_rope = kr_ref[0]
        qs = [jnp.concatenate([qn_ref[0, h], qr_ref[0, h]], axis=-1) for h in heads]
        ks = [jnp.concatenate([kn_ref[0, h], k_rope], axis=-1) for h in heads]
        ss = [lax.dot_general(q, k, _NT, preferred_element_type=F32) for q, k in zip(qs, ks)]
        if masked:
            keep = (lax.broadcasted_iota(jnp.int32, ss[0].shape, 1) <= lax.broadcasted_iota(jnp.int32, ss[0].shape, 0))
            ss = [jnp.where(keep, s, NEG) for s in ss]
        m_old = [m_ref[h] for h in heads]
        m_new = [jnp.maximum(mo, jnp.max(s, axis=-1, keepdims=True)) for mo, s in zip(m_old, ss)]
        alpha = [jnp.exp(mo - mn) for mo, mn in zip(m_old, m_new)]
        ps = [jnp.exp(s - mn) for s, mn in zip(ss, m_new)]
        pv = [jnp.dot(p.astype(BF16), v_ref[0, h], preferred_element_type=F32) for h, p in zip(heads, ps)]
        for h in heads:
            l_ref[h] = alpha[h] * l_ref[h] + jnp.sum(ps[h], axis=-1, keepdims=True)
            acc_ref[h] = alpha[h] * acc_ref[h] + pv[h]
            m_ref[h] = m_new[h]

    @pl.when(ki < qi)
    def _():
        step(False)

    @pl.when(ki == qi)
    def _():
        step(True)
        o_ref[0] = jnp.concatenate([acc_ref[h] / l_ref[h] for h in heads], axis=-1)


def _flash_attention(qn, qr, kn, kr, v):
    b, h, s, _ = qn.shape
    tq = min(512, s)
    nq = s // tq
    hb = 2
    pairs = [(qi, ki) for qi in range(nq) for ki in range(qi + 1)]
    qtab = jnp.asarray([p[0] for p in pairs], jnp.int32)
    ktab = jnp.asarray([p[1] for p in pairs], jnp.int32)
    q_spec = pl.BlockSpec((1, hb, tq, LANE), lambda bi, hi, t, qt, kt: (bi, hi, qt[t], 0))
    k_spec = pl.BlockSpec((1, hb, tq, LANE), lambda bi, hi, t, qt, kt: (bi, hi, kt[t], 0))
    grid_spec = pltpu.PrefetchScalarGridSpec(
        num_scalar_prefetch=2,
        grid=(b, h // hb, len(pairs)),
        in_specs=[q_spec, q_spec, k_spec,
                  pl.BlockSpec((1, tq, LANE), lambda bi, hi, t, qt, kt: (bi, kt[t], 0)), k_spec],
        out_specs=pl.BlockSpec((1, tq, hb * LANE), lambda bi, hi, t, qt, kt: (bi, qt[t], hi)),
        scratch_shapes=[pltpu.VMEM((hb, tq, 1), F32), pltpu.VMEM((hb, tq, 1), F32),
                        pltpu.VMEM((hb, tq, LANE), F32)],
    )
    return pl.pallas_call(
        functools.partial(_flash_body, hb=hb),
        grid_spec=grid_spec,
        out_shape=jax.ShapeDtypeStruct((b, s, h * LANE), F32),
        compiler_params=_cparams(("parallel", "parallel", "arbitrary")),
        name="mla_prompt_attn",
    )(qtab, ktab, qn, qr, kn, kr, v)


def _dec_attn_body(pt_ref, *refs, pps, group, n_steps, t_dec):
    cache_refs = refs[:pps]
    wuk_ref, qp_ref, qr_ref, cnew_ref, krnew_ref, wuv_ref = refs[pps:pps + 6]
    o_ref = refs[pps + 6]
    wq_ref, m_ref, l_ref, acc_ref = refs[pps + 7:]
    del pt_ref
    i = pl.program_id(1)
    n_up = HEADS * D_NOPE
    rows = t_dec * HEADS

    @pl.when(i == 0)
    def _():
        wq_ref[0:n_up, :] = wuk_ref[...]
        wq_ref[n_up:n_up + rows, :] = qp_ref[0]
        m_ref[...] = jnp.full(m_ref.shape, NEG, F32)
        l_ref[...] = jnp.zeros(l_ref.shape, F32)
        acc_ref[...] = jnp.zeros(acc_ref.shape, F32)

    def up_project(ct_bf):
        return jnp.dot(wq_ref[...], ct_bf, preferred_element_type=F32)

    def process(res, ct_bf, krt_bf, mask):
        n = ct_bf.shape[1]
        kn = res[:n_up].reshape(HEADS, D_NOPE, n)
        inv = lax.rsqrt(jnp.sum(kn * kn, axis=1) * (1.0 / D_NOPE) + EPS)
        s = res[n_up:] * jnp.concatenate([inv] * t_dec, axis=0)
        s = s + jnp.dot(qr_ref[0], krt_bf, preferred_element_type=F32)
        if mask is not None:
            s = jnp.where(mask, s, NEG)
        m_old = m_ref[...]
        m_new = jnp.maximum(m_old, jnp.max(s, axis=-1, keepdims=True))
        alpha = jnp.exp(m_old - m_new)
        p = jnp.exp(s - m_new)
        l_ref[...] = alpha * l_ref[...] + jnp.sum(p, axis=-1, keepdims=True)
        pc = lax.dot_general(p.astype(BF16), ct_bf, _NT, preferred_element_type=F32)
        acc_ref[...] = alpha * acc_ref[...] + pc
        m_ref[...] = m_new

    parts = []
    for u in range(0, pps, group):
        lats = [cache_refs[u + g][0, 0] for g in range(group)]
        ct_g = jnp.concatenate([lt[:KV_LORA].astype(BF16) for lt in lats], axis=1)
        krt_g = jnp.concatenate([lt[KV_LORA:LATENT].astype(BF16) for lt in lats], axis=1)
        parts.append((up_project(ct_g), ct_g, krt_g))
    process(*[jnp.concatenate(x, axis=1) if len(x) > 1 else x[0] for x in zip(*parts)], None)

    @pl.when(i == n_steps - 1)
    def _():
        npad = cnew_ref.shape[2]
        key = lax.broadcasted_iota(jnp.int32, (rows, npad), 1)
        tq = lax.broadcasted_iota(jnp.int32, (rows, npad), 0) // HEADS
        process(up_project(cnew_ref[0]), cnew_ref[0], krnew_ref[0], key <= tq)
        o_lat = (acc_ref[...] / l_ref[...]).astype(BF16)
        full = jnp.dot(o_lat, wuv_ref[...], preferred_element_type=F32)
        rh = lax.broadcasted_iota(jnp.int32, full.shape, 0) % HEADS
        ch = lax.broadcasted_iota(jnp.int32, full.shape, 1) // LANE
        diag = jnp.where(rh == ch, full, 0.0).reshape(t_dec, HEADS, full.shape[1])
        o_ref[0] = jnp.sum(diag, axis=1)


def _dec_attention(cache, page_table, wuk_t, qp, qr, cnew, krnew, wuv, layer):
    bdec, n_pages = page_table.shape
    rows = qp.shape[1]
    t_dec = rows // HEADS
    pps = 16 if n_pages % 16 == 0 else 2
    group = min(pps, DEC_GROUP)
    n_steps = n_pages // pps
    n_up = HEADS * D_NOPE

    def cache_spec(u):
        return pl.BlockSpec((1, 1, LATENT, PAGE), lambda b, i, pt: (layer, pt[b, i * pps + u], 0, 0))

    body = functools.partial(_dec_attn_body, pps=pps, group=group, n_steps=n_steps, t_dec=t_dec)
    grid_spec = pltpu.PrefetchScalarGridSpec(
        num_scalar_prefetch=1,
        grid=(bdec, n_steps),
        in_specs=[
            *[cache_spec(u) for u in range(pps)],
            pl.BlockSpec((n_up, KV_LORA), lambda b, i, pt: (0, 0)),
            pl.BlockSpec((1, rows, KV_LORA), lambda b, i, pt: (b, 0, 0)),
            pl.BlockSpec((1, rows, D_ROPE), lambda b, i, pt: (b, 0, 0)),
            pl.BlockSpec((1, KV_LORA, PAGE), lambda b, i, pt: (b, 0, 0)),
            pl.BlockSpec((1, D_ROPE, PAGE), lambda b, i, pt: (b, 0, 0)),
            pl.BlockSpec((KV_LORA, n_up), lambda b, i, pt: (0, 0)),
        ],
        out_specs=pl.BlockSpec((1, t_dec, n_up), lambda b, i, pt: (b, 0, 0)),
        scratch_shapes=[
            pltpu.VMEM((n_up + rows, KV_LORA), BF16),
            pltpu.VMEM((rows, 1), F32),
            pltpu.VMEM((rows, 1), F32),
            pltpu.VMEM((rows, KV_LORA), F32),
        ],
    )
    return pl.pallas_call(
        body,
        grid_spec=grid_spec,
        out_shape=jax.ShapeDtypeStruct((bdec, t_dec, n_up), F32),
        compiler_params=_cparams(("parallel", "arbitrary")),
        name="mla_sample_attn",
    )(page_table, *([cache] * pps), wuk_t, qp, qr, cnew, krnew, wuv)


def _absorb_body(qn_ref, g_ref, wuk_ref, o_ref):
    q = (qn_ref[...] * g_ref[...]).astype(BF16)
    o_ref[0] = lax.dot_general(q, wuk_ref[...], _NT, preferred_element_type=F32).astype(o_ref.dtype)


def _absorb_q(qn, g_kn, wuk):
    m = qn.shape[0]
    return pl.pallas_call(
        _absorb_body,
        grid=(HEADS,),
        in_specs=[
            pl.BlockSpec((m, LANE), lambda h: (0, h)),
            pl.BlockSpec((1, LANE), lambda h: (0, 0)),
            pl.BlockSpec((KV_LORA, LANE), lambda h: (0, h)),
        ],
        out_specs=pl.BlockSpec((1, m, KV_LORA), lambda h: (h, 0, 0)),
        out_shape=jax.ShapeDtypeStruct((HEADS, m, KV_LORA), BF16),
        compiler_params=_cparams(("parallel",)),
        name="mla_absorb_q",
    )(qn, g_kn, wuk)


def _rope_tables(pos):
    inv = ROPE_THETA ** (-jnp.arange(0, D_ROPE, 2, dtype=F32) / D_ROPE)
    ang = pos.astype(F32)[:, None] * inv[None, :]
    c, s = jnp.cos(ang), jnp.sin(ang)
    return jnp.tile(c, (1, 4)), jnp.concatenate([-s, s, -s, s], axis=1)


def _mla_weights(w_in, g_qn, g_qr, g_kv, g_kr, g_kn, w_uk, w_uv, w_o):
    d = w_in.shape[0]
    q_dim = HEADS * D_QK
    wq = w_in[:, :q_dim].reshape(d, HEADS, D_QK)
    w_kr = w_in[:, q_dim + KV_LORA:q_dim + LATENT]
    w_kr_sw = jnp.concatenate([w_kr[:, D_ROPE // 2:], w_kr[:, :D_ROPE // 2]], axis=1)
    g_kr_sw = jnp.concatenate([g_kr[D_ROPE // 2:], g_kr[:D_ROPE // 2]])
    wuk2 = w_uk.reshape(KV_LORA, HEADS * D_NOPE)
    return dict(
        w_qn=wq[:, :, :D_NOPE].reshape(d, HEADS * D_NOPE).astype(BF16),
        w_qr=wq[:, :, D_NOPE:].reshape(d, HEADS * D_ROPE).astype(BF16),
        w_lat=jnp.concatenate([w_in[:, q_dim:q_dim + KV_LORA], w_kr, w_kr, w_kr_sw, w_kr_sw], axis=1).astype(BF16),
        w_gate=w_in[:, q_dim + LATENT:].astype(BF16),
        g_qn=g_qn.reshape(1, D_NOPE),
        g_qr2=jnp.tile(g_qr, 2).reshape(1, LANE),
        g_kv=g_kv.reshape(1, KV_LORA),
        g_kr2=jnp.tile(g_kr, 2).reshape(1, LANE),
        g_kr2p=jnp.tile(g_kr_sw, 2).reshape(1, LANE),
        g_kn=g_kn.reshape(1, D_NOPE),
        wuk=wuk2.astype(BF16),
        wuk_t=wuk2.T.astype(BF16),
        wuv=w_uv.reshape(KV_LORA, HEADS * D_NOPE).astype(BF16),
        w_o=w_o.astype(BF16),
    )


def _mla_project(x2, g_norm, wts, c4, s4, seq_4d):
    m, d = x2.shape
    tm = _pick_tm(m)
    four_d = seq_4d is not None
    n_tab = c4.shape[0] // tm
    tab_spec = pl.BlockSpec((tm, LANE), lambda i, j: (i % n_tab, 0))
    lane_vec = pl.BlockSpec((1, LANE), lambda i, j: (0, 0))
    pro_args, pro_specs = [x2, g_norm], [_row_spec(tm, d), _vec_spec(d)]

    def q_out(n_heads_blk, n_cols):
        if four_d:
            nsb = seq_4d // tm
            shape = jax.ShapeDtypeStruct((m // seq_4d, HEADS, seq_4d, LANE), BF16)
            spec = pl.BlockSpec((1, n_heads_blk, tm, LANE), lambda i, j: (i // nsb, j, i % nsb, 0))
        else:
            shape = jax.ShapeDtypeStruct((m, n_cols), F32)
            spec = _tile_spec(tm, n_cols)
        return [shape], [spec]

    shp, spc = q_out(HEADS, HEADS * D_NOPE)
    qn = _mm("mla_qnope", _pro_rms, pro_args, pro_specs, wts["w_qn"],
             functools.partial(_epi_qnope, four_d=four_d), [wts["g_qn"]], [lane_vec], shp, spc, tm,
             HEADS * D_NOPE)[0]
    shp, spc = q_out(HEADS, HEADS * D_ROPE)
    qr = _mm("mla_qrope", _pro_rms, pro_args, pro_specs, wts["w_qr"],
             functools.partial(_epi_qrope, four_d=four_d), [wts["g_qr2"], c4, s4],
             [lane_vec, tab_spec, tab_spec], shp, spc, tm, HEADS * D_ROPE)[0]
    n_lat = wts["w_lat"].shape[1]
    lat, cn, kr = _mm(
        "mla_latent", _pro_rms, pro_args, pro_specs, wts["w_lat"], _epi_latent,
        [wts["g_kv"], wts["g_kr2"], wts["g_kr2p"], c4, s4],
        [pl.BlockSpec((1, KV_LORA), lambda i, j: (0, 0)), lane_vec, lane_vec, tab_spec, tab_spec],
        [jax.ShapeDtypeStruct((m, LATENT), F32), jax.ShapeDtypeStruct((m, KV_LORA), BF16),
         jax.ShapeDtypeStruct((m, LANE), BF16)],
        [pl.BlockSpec((tm, LATENT), lambda i, j: (i, 0)), pl.BlockSpec((tm, KV_LORA), lambda i, j: (i, 0)),
         pl.BlockSpec((tm, LANE), lambda i, j: (i, 0))],
        tm, n_lat)
    gate = _mm_simple("mla_gate", _pro_rms, pro_args, ["row", "vec"], wts["w_gate"])
    return qn, qr, lat, cn, kr, gate


def _mla_prompt(x, g_norm, wts, c4, s4):
    b, s, d = x.shape
    x2 = x.reshape(b * s, d)
    tm = _pick_tm(b * s)
    nsb = s // tm
    qn, qr, lat, cn, kr, gate = _mla_project(x2, g_norm, wts, c4, s4, s)
    head_shape = [jax.ShapeDtypeStruct((b, HEADS, s, LANE), BF16)]
    head_spec = [pl.BlockSpec((1, HEADS, tm, LANE), lambda i, j: (i // nsb, j, i % nsb, 0))]
    kn = _mm("mla_knope", _pro_cast, [cn], [_row_spec(tm, KV_LORA)], wts["wuk"], _epi_knope, [wts["g_kn"]],
             [pl.BlockSpec((1, LANE), lambda i, j: (0, 0))], head_shape, head_spec, tm, HEADS * D_NOPE)[0]
    v = _mm("mla_value", _pro_cast, [cn], [_row_spec(tm, KV_LORA)], wts["wuv"], _epi_heads, [], [],
            head_shape, head_spec, tm, HEADS * D_NOPE)[0]
    o = _flash_attention(qn, qr, kn, kr.reshape(b, s, LANE), v).reshape(b * s, HEADS * D_NOPE)
    y = _mm_simple("mla_out", _pro_gate, [o, gate], ["row", "row"], wts["w_o"], _epi_res, [x2], ["tile"])
    return y.reshape(b, s, d), lat.reshape(b, s, LATENT)


def _mla_sample(x, g_norm, wts, c4, s4, cache, page_table, layer):
    b, t, d = x.shape
    m = b * t
    x2 = x.reshape(m, d)
    qn, qr, lat, cn, kr, gate = _mla_project(x2, g_norm, wts, c4, s4, None)
    qp = _absorb_q(qn, wts["g_kn"], wts["wuk"])
    qp = qp.reshape(HEADS, b, t, KV_LORA).transpose(1, 2, 0, 3).reshape(b, t * HEADS, KV_LORA)
    qr = qr.reshape(b, t * HEADS, D_ROPE).astype(BF16)
    cnew = jnp.pad(cn.reshape(b, t, KV_LORA).transpose(0, 2, 1), ((0, 0), (0, 0), (0, PAGE - t)))
    krnew = jnp.pad(kr[:, :D_ROPE].reshape(b, t, D_ROPE).transpose(0, 2, 1), ((0, 0), (0, 0), (0, PAGE - t)))
    o = _dec_attention(cache, page_table, wts["wuk_t"], qp, qr, cnew, krnew, wts["wuv"], layer)
    o = o.reshape(m, HEADS * D_NOPE)
    y = _mm_simple("mla_out_s", _pro_gate, [o, gate], ["row", "row"], wts["w_o"], _epi_res, [x2], ["tile"])
    return y.reshape(b, t, d), lat.reshape(b, t, LATENT)


def _s5_body(u_ref, wb_ref, wc_ref, a_ref, apow_ref, d_ref, x0_ref, g_ref, xl_ref, bu_ref, carry_ref,
             *, seq, rows_per_seq_tile):
    tc = u_ref.shape[0]
    ns = S5_GB * S5_STATE
    long_seq = seq >= tc
    u = u_ref[...]
    bu_ref[...] = jnp.dot(u.astype(BF16), wb_ref[0], preferred_element_type=F32)
    a_re, a_im = a_ref[0, 0:1, :], a_ref[0, 1:2, :]

    def cmul_add(xr, xi, pr, pi, sr, si):
        return xr + pr * sr - pi * si, xi + pr * si + pi * sr

    if long_seq:
        t_idx = pl.program_id(1) % (seq // tc)

        @pl.when(t_idx == 0)
        def _():
            carry_ref[...] = x0_ref[0]

        row = lax.broadcasted_iota(jnp.int32, (8, ns), 0)
        pw = [(jnp.where(row >= k, apow_ref[0, 0, k - 1:k, :], 0.0), jnp.where(row >= k, apow_ref[0, 1, k - 1:k, :], 0.0))
              for k in (1, 2, 4)]
        ap_re, ap_im = apow_ref[0, 0], apow_ref[0, 1]

        def tile_step(j, carry):
            cr, ci = carry
            r0 = pl.multiple_of(j * 8, 8)
            xr = bu_ref[pl.ds(r0, 8), 0:ns]
            xi = bu_ref[pl.ds(r0, 8), ns:2 * ns]
            for k, (pr, pi) in zip((1, 2, 4), pw):
                xr, xi = cmul_add(xr, xi, pr, pi, pltpu.roll(xr, k, 0), pltpu.roll(xi, k, 0))
            xr, xi = cmul_add(xr, xi, ap_re, ap_im, cr, ci)
            bu_ref[pl.ds(r0, 8), 0:ns] = xr
            bu_ref[pl.ds(r0, 8), ns:2 * ns] = xi
            return xr[7:8, :], xi[7:8, :]

        cr, ci = lax.fori_loop(0, tc // 8, tile_step, (carry_ref[0:1, :], carry_ref[1:2, :]))
        carry_ref[0:1, :] = cr
        carry_ref[1:2, :] = ci
        xl_ref[0, 0:1, :] = cr
        xl_ref[0, 1:2, :] = ci
    else:
        xr = bu_ref[:, 0:ns]
        xi = bu_ref[:, ns:2 * ns]
        pos = lax.broadcasted_iota(jnp.int32, (tc, ns), 0) % seq
        k = 1
        pr, pi = a_re, a_im
        while k < seq:
            keep = pos >= k
            sr = jnp.where(keep, pltpu.roll(xr, k, 0), 0.0)
            si = jnp.where(keep, pltpu.roll(xi, k, 0), 0.0)
            xr, xi = cmul_add(xr, xi, pr, pi, sr, si)
            pr, pi = pr * pr - pi * pi, 2.0 * pr * pi
            k *= 2
        ap_re = jnp.concatenate([apow_ref[0, 0, 0:seq, :]] * (tc // seq), axis=0)
        ap_im = jnp.concatenate([apow_ref[0, 1, 0:seq, :]] * (tc // seq), axis=0)
        xr, xi = cmul_add(xr, xi, ap_re, ap_im, x0_ref[:, 0:ns], x0_ref[:, ns:2 * ns])
        bu_ref[:, 0:ns] = xr
        bu_ref[:, ns:2 * ns] = xi
        xl_ref[...] = bu_ref[...]

    y = jnp.dot(bu_ref[...].astype(BF16), wc_ref[0], preferred_element_type=F32) + d_ref[...] * u
    g_ref[...] = _gelu_tanh(y)


def _s5_discretize(lam_re, lam_im, log_dt, b_re, b_im, c_re, c_im):
    g, n = lam_re.shape
    dt = jnp.exp(log_dt)[:, None]
    mag = jnp.exp(lam_re * dt)
    ab_re, ab_im = mag * jnp.cos(lam_im * dt), mag * jnp.sin(lam_im * dt)
    nr, ni = ab_re - 1.0, ab_im
    den = lam_re * lam_re + lam_im * lam_im
    fr, fi = (nr * lam_re + ni * lam_im) / den, (ni * lam_re - nr * lam_im) / den
    bb_re = fr[..., None] * b_re - fi[..., None] * b_im
    bb_im = fr[..., None] * b_im + fi[..., None] * b_re
    nblk = g // S5_GB
    eye = jnp.eye(S5_GB, dtype=F32)

    def expand_b(bb):
        bb = bb.reshape(nblk, S5_GB, n, S5_GROUP)
        return jnp.einsum("bgnj,gh->bgjhn", bb, eye).reshape(nblk, S5_GB * S5_GROUP, S5_GB * n)

    def expand_c(cc):
        cc = cc.reshape(nblk, S5_GB, S5_GROUP, n)
        return jnp.einsum("bgjn,gh->bgnhj", cc, eye).reshape(nblk, S5_GB * n, S5_GB * S5_GROUP)

    wb = jnp.concatenate([expand_b(bb_re), expand_b(bb_im)], axis=2).astype(BF16)
    wc = jnp.concatenate([expand_c(c_re), -expand_c(c_im)], axis=1).astype(BF16)
    kk = jnp.arange(1, 9, dtype=F32)[:, None, None]
    magk = jnp.exp(kk * lam_re * dt)
    pk_re, pk_im = magk * jnp.cos(kk * lam_im * dt), magk * jnp.sin(kk * lam_im * dt)
    apow = jnp.stack([pk_re, pk_im], axis=0).reshape(2, 8, nblk, S5_GB * n).transpose(2, 0, 1, 3)
    a1 = jnp.stack([ab_re, ab_im], axis=0).reshape(2, nblk, S5_GB * n).transpose(1, 0, 2)
    return wb, wc, a1, apow


def _s5_mix(u_src, col0, x0_re, x0_im, prm, seq):
    wb, wc, a1, apow, d_skip = prm
    m = u_src.shape[0]
    nblk = wb.shape[0]
    cb = S5_GB * S5_GROUP
    ns = S5_GB * S5_STATE
    width = nblk * cb
    nseq = m // seq
    tc = _pick_tm(m) if seq >= 8 else m
    long_seq = seq >= tc
    n_tiles = m // tc
    cblk0 = col0 // cb
    x0 = jnp.concatenate([x0_re.reshape(nseq, nblk, ns), x0_im.reshape(nseq, nblk, ns)], axis=2)
    if long_seq:
        tiles_per_seq = seq // tc
        x0_arr = x0.transpose(1, 0, 2).reshape(nblk * nseq, 2, ns)
        x0_spec = pl.BlockSpec((1, 2, ns), lambda gb, r: (gb * nseq + r // tiles_per_seq, 0, 0))
        xl_shape = jax.ShapeDtypeStruct((nblk * nseq, 2, ns), F32)
        xl_spec = pl.BlockSpec((1, 2, ns), lambda gb, r: (gb * nseq + r // tiles_per_seq, 0, 0))
        carry_shape = (2, ns)
    else:
        x0_rows = jnp.repeat(x0, seq, axis=0)
        x0_arr = x0_rows.reshape(m, nblk * 2 * ns)
        x0_spec = pl.BlockSpec((tc, 2 * ns), lambda gb, r: (r, gb))
        xl_shape = jax.ShapeDtypeStruct((m, nblk * 2 * ns), F32)
        xl_spec = pl.BlockSpec((tc, 2 * ns), lambda gb, r: (r, gb))
        carry_shape = (2, ns)
    body = functools.partial(_s5_body, seq=seq, rows_per_seq_tile=tc)
    g, xl = pl.pallas_call(
        body,
        grid=(nblk, n_tiles),
        in_specs=[
            pl.BlockSpec((tc, cb), lambda gb, r: (r, cblk0 + gb)),
            pl.BlockSpec((1, cb, 2 * ns), lambda gb, r: (gb, 0, 0)),
            pl.BlockSpec((1, 2 * ns, cb), lambda gb, r: (gb, 0, 0)),
            pl.BlockSpec((1, 2, ns), lambda gb, r: (gb, 0, 0)),
            pl.BlockSpec((1, 2, 8, ns), lambda gb, r: (gb, 0, 0, 0)),
            pl.BlockSpec((1, cb), lambda gb, r: (0, gb)),
            x0_spec,
        ],
        out_specs=[pl.BlockSpec((tc, cb), lambda gb, r: (r, gb)), xl_spec],
        out_shape=[jax.ShapeDtypeStruct((m, width), F32), xl_shape],
        scratch_shapes=[pltpu.VMEM((tc, 2 * ns), F32), pltpu.VMEM(carry_shape, F32)],
        compiler_params=_cparams(("parallel", "arbitrary")),
        name="s5_mix",
    )(u_src, wb, wc, a1, apow, d_skip, x0_arr)
    if long_seq:
        xl = xl.reshape(nblk, nseq, 2, ns).transpose(2, 1, 0, 3).reshape(2, nseq, nblk * ns)
    else:
        xl = xl.reshape(nseq, seq, nblk, 2, ns)[:, seq - 1].transpose(2, 0, 1, 3).reshape(2, nseq, nblk * ns)
    return g, xl[0], xl[1]


def _s5_layer(x, g_norm, x0_re, x0_im, w_in, mix_prm, w_glu, b_glu, w_o):
    b, t, d = x.shape
    m = b * t
    x2 = x.reshape(m, d)
    width = w_glu.shape[0]
    z = _mm_simple("s5_in", _pro_rms, [x2, g_norm], ["row", "vec"], w_in)
    g, xr, xi = _s5_mix(z, 0, x0_re, x0_im, mix_prm, t)
    g2 = _mm_simple("s5_glu", _pro_cast, [g], ["row"], w_glu, _epi_glu, [g, b_glu], ["tile", "nvec"])
    tm = 256 if m % 256 == 0 else m
    gate_spec = pl.BlockSpec((tm, width), lambda i, j: (i, 1))
    y = _mm("s5_out", _pro_gate, [g2, z], [_row_spec(tm, width), gate_spec], w_o, _epi_res, [x2],
            [_tile_spec(tm, d)], [jax.ShapeDtypeStruct((m, d), F32)], [_tile_spec(tm, d)], tm, d)[0]
    ng = width // S5_GROUP
    return y.reshape(b, t, d), xr.reshape(b, ng, S5_STATE), xi.reshape(b, ng, S5_STATE)


def _rwkv_body(r_ref, k_ref, v_ref, dw_ref, da_ref, w0_ref, a0_ref, kk_ref, ka_ref, rk_ref, gw_ref, gb_ref,
               s0_ref, y_ref, sout_ref, state_ref, *, t_valid, n_chunks, pairs):
    lc = r_ref.shape[1]
    l2 = 2 * lc
    ci = pl.program_id(2)
    prs = range(pairs)

    def tile(x, p):
        return x[:, p * LANE:(p + 1) * LANE]

    @pl.when(ci == 0)
    def _():
        zero = jnp.zeros((RW_HEAD, RW_HEAD), F32)
        for p in prs:
            top = jnp.concatenate([s0_ref[0, 2 * p], zero], axis=1)
            bot = jnp.concatenate([zero, s0_ref[0, 2 * p + 1]], axis=1)
            state_ref[p] = jnp.concatenate([top, bot], axis=0)

    lane = lax.broadcasted_iota(jnp.int32, (lc, LANE), 1)
    lo = lane < RW_HEAD
    row = lax.broadcasted_iota(jnp.int32, (l2, l2), 0)
    col = lax.broadcasted_iota(jnp.int32, (l2, l2), 1)
    same = (row // lc) == (col // lc)
    strict = same & ((col % lc) < (row % lc))
    incl = same & ((col % lc) <= (row % lc))
    eye = (col == row).astype(F32)
    r1 = lax.broadcasted_iota(jnp.int32, (lc, lc), 0)
    c1 = lax.broadcasted_iota(jnp.int32, (lc, lc), 1)
    tri_incl = (c1 <= r1).astype(F32)

    def seg_sum(x):
        se = jnp.sum(jnp.where(lo, x, 0.0), axis=-1, keepdims=True)
        so = jnp.sum(jnp.where(lo, 0.0, x), axis=-1, keepdims=True)
        return jnp.where(lo, se, so)

    def stack(x):
        return jnp.concatenate([jnp.where(lo, x, 0.0), jnp.where(lo, 0.0, x)], axis=0)

    r = r_ref[0]
    k = k_ref[0]
    v = v_ref[0]
    wl = -_softplus(-(w0_ref[...] + dw_ref[0])) - 0.5
    lw = -jnp.exp(wl)
    a_rate = _sigmoid(a0_ref[...] + da_ref[0])
    kkr = k * kk_ref[...]
    kf = k * (1.0 + (a_rate - 1.0) * ka_ref[...])
    rkf = r * kf * rk_ref[...]
    if t_valid < lc:
        valid = lax.broadcasted_iota(jnp.int32, lw.shape, 0) < t_valid
        lw = jnp.where(valid, lw, 0.0)
        kkr = jnp.where(valid, kkr, 0.0)
        kf = jnp.where(valid, kf, 0.0)
    cl = jnp.dot(tri_incl, lw, preferred_element_type=F32, precision=lax.Precision.HIGHEST)
    p_in = jnp.exp(cl)
    p_ex = jnp.exp(cl - lw)
    p_inv = jnp.exp(-cl)
    p_end = jnp.exp(cl[lc - 1:lc, :] - cl)
    rt_full = r * p_in

    kk = [tile(kkr, p) for p in prs]
    kk = [x / jnp.maximum(jnp.sqrt(seg_sum(x * x)), 1e-12) for x in kk]
    bv = [kk[p] * tile(a_rate, p) for p in prs]
    at_st = [stack(-kk[p] * tile(p_ex, p)).astype(BF16) for p in prs]
    rt_st = [stack(tile(rt_full, p)).astype(BF16) for p in prs]
    bt_st = [stack(bv[p] * tile(p_inv, p)).astype(BF16) for p in prs]
    kt_st = [stack(tile(kf, p) * tile(p_inv, p)).astype(BF16) for p in prs]
    v_st = [stack(tile(v, p)).astype(BF16) for p in prs]
    bp_st = [stack(bv[p] * tile(p_end, p)).astype(BF16) for p in prs]
    kp_st = [stack(tile(kf, p) * tile(p_end, p)).astype(BF16) for p in prs]

    def nt(a, b):
        return lax.dot_general(a, b, _NT, preferred_element_type=F32)

    def nn(a, b):
        return jnp.dot(a.astype(BF16), b.astype(BF16), preferred_element_type=F32)

    if l2 % LANE == 0:
        amat = [nt(jnp.concatenate([at_st[p], rt_st[p]], axis=0), jnp.concatenate([bt_st[p], kt_st[p]], axis=0))
                for p in prs]
        n_ab = [jnp.where(strict, m[:l2, :l2], 0.0) for m in amat]
        a_ak = [jnp.where(strict, m[:l2, l2:], 0.0) for m in amat]
        a_rb = [jnp.where(incl, m[l2:, :l2], 0.0) for m in amat]
        a_rk = [jnp.where(incl, m[l2:, l2:], 0.0) for m in amat]
    else:
        n_ab = [jnp.where(strict, nt(at_st[p], bt_st[p]), 0.0) for p in prs]
        a_ak = [jnp.where(strict, nt(at_st[p], kt_st[p]), 0.0) for p in prs]
        a_rb = [jnp.where(incl, nt(rt_st[p], bt_st[p]), 0.0) for p in prs]
        a_rk = [jnp.where(incl, nt(rt_st[p], kt_st[p]), 0.0) for p in prs]

    tinv = [eye + n for n in n_ab]
    npow = n_ab
    span = 2
    while span < min(lc, t_valid):
        npow = [nn(n, n) for n in npow]
        tinv = [t + nn(n, t) for n, t in zip(npow, tinv)]
        span *= 2

    s_old = [state_ref[p] for p in prs]
    s_bf = [x.astype(BF16) for x in s_old]
    rhs_u = [nt(at_st[p], s_bf[p]) + nn(a_ak[p], v_st[p]) for p in prs]
    u_bf = [nn(tinv[p], rhs_u[p]).astype(BF16) for p in prs]
    y_st = [nt(rt_st[p], s_bf[p]) + nn(a_rb[p], u_bf[p]) + nn(a_rk[p], v_st[p]) for p in prs]
    for p in prs:
        s_new = s_old[p] * tile(p_in, p)[lc - 1:lc, :]
        s_new = s_new + lax.dot_general(u_bf[p], bp_st[p], _TN, preferred_element_type=F32)
        s_new = s_new + lax.dot_general(v_st[p], kp_st[p], _TN, preferred_element_type=F32)
        state_ref[p] = s_new

    inv_n = 1.0 / RW_HEAD
    outs = []
    for p in prs:
        y = y_st[p][:lc] + y_st[p][lc:]
        yc = y - seg_sum(y) * inv_n
        var = seg_sum(yc * yc) * inv_n
        yn = yc * lax.rsqrt(var + GN_EPS) * tile(gw_ref[...], p) + tile(gb_ref[...], p)
        outs.append(yn + seg_sum(tile(rkf, p)) * tile(v, p))
    y_ref[0] = jnp.concatenate(outs, axis=-1)

    @pl.when(ci == n_chunks - 1)
    def _():
        for p in prs:
            sbd = state_ref[p]
            sout_ref[0, 2 * p] = sbd[:RW_HEAD, :RW_HEAD]
            sout_ref[0, 2 * p + 1] = sbd[RW_HEAD:, RW_HEAD:]


def _rwkv_core(r, k, v, dw, da, vecs, s0, t_valid):
    b, t, w = r.shape
    lc = min(RW_CHUNK, t)
    n_chunks = t // lc
    pairs = min(w // LANE, 16)
    wb = pairs * LANE
    nhb = w // wb
    hpb = 2 * pairs
    seq_spec = pl.BlockSpec((1, lc, wb), lambda bi, hi, ci: (bi, ci, hi))
    vec_spec = pl.BlockSpec((1, wb), lambda bi, hi, ci: (0, hi))
    st_spec = pl.BlockSpec((1, hpb, RW_HEAD, RW_HEAD), lambda bi, hi, ci: (bi, hi, 0, 0))
    body = functools.partial(_rwkv_body, t_valid=t_valid, n_chunks=n_chunks, pairs=pairs)
    return pl.pallas_call(
        body,
        grid=(b, nhb, n_chunks),
        in_specs=[seq_spec] * 5 + [vec_spec] * 7 + [st_spec],
        out_specs=[seq_spec, st_spec],
        out_shape=[jax.ShapeDtypeStruct((b, t, w), F32), jax.ShapeDtypeStruct(s0.shape, F32)],
        scratch_shapes=[pltpu.VMEM((pairs, LANE, LANE), F32)],
        compiler_params=_cparams(("parallel", "parallel", "arbitrary")),
        name="rwkv7_core",
    )(r, k, v, dw, da, *vecs, s0)


def _rwkv_layer(x, g_norm, shift0, s0, wts):
    b, t, d = x.shape
    m = b * t
    x2 = x.reshape(m, d)
    h = _rms_rows(x2, g_norm)
    h3 = h.reshape(b, t, d)
    prev = jnp.concatenate([shift0[:, None, :], h3[:, :-1]], axis=1).reshape(m, d)
    mu = wts["mu"]

    def mixed(name, n, w):
        return _mm_simple(name, _pro_mix, [h, prev, mu[n:n + 1]], ["row", "row", "vec"], w)

    r = mixed("rwkv_r", 0, wts["w_r"])
    k = mixed("rwkv_k", 2, wts["w_k"])
    v = mixed("rwkv_v", 3, wts["w_v"])
    gpre = mixed("rwkv_g", 5, wts["w_g"])
    dw = _mm_simple("rwkv_w2", _pro_tanh, [mixed("rwkv_w1", 1, wts["w1"])], ["row"], wts["w2"])
    da = _mm_simple("rwkv_a2", _pro_cast, [mixed("rwkv_a1", 4, wts["a1"])], ["row"], wts["a2"])
    tp = t if t % 8 == 0 else 8 * ((t + 7) // 8)

    def seq3(a):
        a = a.reshape(b, t, -1)
        return a if tp == t else jnp.pad(a, ((0, 0), (0, tp - t), (0, 0)))

    y, s_fin = _rwkv_core(seq3(r), seq3(k), seq3(v), seq3(dw), seq3(da), wts["vecs"], s0, t)
    y = y[:, :t].reshape(m, -1)
    out = _mm_simple("rwkv_out", _pro_gate, [y, gpre], ["row", "row"], wts["w_o"], _epi_res, [x2], ["tile"])
    return out.reshape(b, t, d), h3[:, -1], s_fin


def _rwkv_weights(mu, w_r, w_k, w_v, w_g, w_o, w0, w1, w2, a0, a1, a2, k_k, k_a, r_k, gn_w, gn_b):
    lora = w1.shape[1]
    lpad = LANE * ((lora + LANE - 1) // LANE)

    def pad_cols(w):
        return jnp.pad(w, ((0, 0), (0, lpad - lora))).astype(BF16)

    def pad_rows(w):
        return jnp.pad(w, ((0, lpad - lora), (0, 0))).astype(BF16)

    row = lambda a: a.reshape(1, -1)
    return dict(
        mu=mu, w_r=w_r.astype(BF16), w_k=w_k.astype(BF16), w_v=w_v.astype(BF16), w_g=w_g.astype(BF16),
        w_o=w_o.astype(BF16), w1=pad_cols(w1), w2=pad_rows(w2), a1=pad_cols(a1), a2=pad_rows(a2),
        vecs=[row(w0), row(a0), row(k_k), row(k_a), row(r_k), row(gn_w), row(gn_b)],
    )


def kernel(x_prompt, x_sample, cache_mla, page_table, state_s5_re, state_s5_im, state_rwkv, state_rwkv_shift, norm_g, mla_w_in, mla_g_qn, mla_g_qr, mla_g_kv, mla_g_kr, mla_g_kn, mla_w_uk, mla_w_uv, mla_w_o, s5_w_in, s5_lam_re, s5_lam_im, s5_log_dt, s5_b_re, s5_b_im, s5_c_re, s5_c_im, s5_d, s5_w_glu, s5_b_glu, s5_w_o, rwkv_mu, rwkv_w_r, rwkv_w_k, rwkv_w_v, rwkv_w_g, rwkv_w_o, rwkv_w0, rwkv_w1, rwkv_w2, rwkv_a0, rwkv_a1, rwkv_a2, rwkv_k_k, rwkv_k_a, rwkv_r_k, rwkv_gn_w, rwkv_gn_b):
    depth = norm_g.shape[0]
    nb, seq, d = x_prompt.shape
    db, t_dec, _ = x_sample.shape
    past = page_table.shape[1] * PAGE
    c4p, s4p = _rope_tables(jnp.arange(seq, dtype=jnp.int32))
    c4s, s4s = _rope_tables(past + jnp.arange(t_dec, dtype=jnp.int32))
    c4s, s4s = jnp.tile(c4s, (db, 1)), jnp.tile(s4s, (db, 1))
    cache_t = jnp.swapaxes(cache_mla, 2, 3)
    yp, ys = x_prompt, x_sample
    mla_p, mla_s = [], []
    s5r_p, s5i_p, s5r_s, s5i_s = [], [], [], []
    rw_p, rw_s, sh_p, sh_s = [], [], [], []
    for i in range(depth):
        kind, j = i % 3, i // 3
        g_norm = norm_g[i].reshape(1, d)
        if kind == 0:
            wts = _mla_weights(mla_w_in[j], mla_g_qn[j], mla_g_qr[j], mla_g_kv[j], mla_g_kr[j], mla_g_kn[j],
                               mla_w_uk[j], mla_w_uv[j], mla_w_o[j])
            yp, lat = _mla_prompt(yp, g_norm, wts, c4p, s4p)
            mla_p.append(lat)
            ys, lat = _mla_sample(ys, g_norm, wts, c4s, s4s, cache_t, page_table, j)
            mla_s.append(lat)
        elif kind == 1:
            wb, wc, a1, apow = _s5_discretize(s5_lam_re[j], s5_lam_im[j], s5_log_dt[j], s5_b_re[j], s5_b_im[j],
                                              s5_c_re[j], s5_c_im[j])
            prm = (wb, wc, a1, apow, s5_d[j].reshape(1, -1))
            args = (s5_w_in[j].astype(BF16), prm, s5_w_glu[j].astype(BF16), s5_b_glu[j].reshape(1, -1),
                    s5_w_o[j].astype(BF16))
            ng, ns = s5_lam_re.shape[1], s5_lam_re.shape[2]
            zero = jnp.zeros((nb, ng, ns), F32)
            yp, xr, xi = _s5_layer(yp, g_norm, zero, zero, *args)
            s5r_p.append(xr)
            s5i_p.append(xi)
            ys, xr, xi = _s5_layer(ys, g_norm, state_s5_re[j], state_s5_im[j], *args)
            s5r_s.append(xr)
            s5i_s.append(xi)
        else:
            wts = _rwkv_weights(rwkv_mu[j], rwkv_w_r[j], rwkv_w_k[j], rwkv_w_v[j], rwkv_w_g[j], rwkv_w_o[j],
                                rwkv_w0[j], rwkv_w1[j], rwkv_w2[j], rwkv_a0[j], rwkv_a1[j], rwkv_a2[j],
                                rwkv_k_k[j], rwkv_k_a[j], rwkv_r_k[j].reshape(-1), rwkv_gn_w[j], rwkv_gn_b[j])
            nh = rwkv_r_k.shape[1]
            sh0 = jnp.zeros((nb, d), F32)
            st0 = jnp.zeros((nb, nh, RW_HEAD, RW_HEAD), F32)
            yp, sh, st = _rwkv_layer(yp, g_norm, sh0, st0, wts)
            sh_p.append(sh)
            rw_p.append(st)
            ys, sh, st = _rwkv_layer(ys, g_norm, state_rwkv_shift[j], state_rwkv[j], wts)
            sh_s.append(sh)
            rw_s.append(st)
    return (yp, ys, jnp.stack(mla_p), jnp.stack(mla_s),
            jnp.stack(s5r_p), jnp.stack(s5i_p), jnp.stack(s5r_s), jnp.stack(s5i_s),
            jnp.stack(rw_p), jnp.stack(rw_s), jnp.stack(sh_p), jnp.stack(sh_s))
```

```python
import functools
import math

import jax
import jax.numpy as jnp
from jax import lax
from jax.experimental import pallas as pl
from jax.experimental.pallas import tpu as pltpu

F32 = jnp.float32
BF16 = jnp.bfloat16

EPS = 1e-6
PAGE = 128
HEADS = 16
D_NOPE = 128
D_ROPE = 64
D_QK = D_NOPE + D_ROPE
KV_LORA = 512
LATENT = KV_LORA + D_ROPE
ROPE_THETA = 10000.0
SM_SCALE = D_QK ** -0.5
NEG = -1e30
S5_GROUP = 16
S5_STATE = 64
S5_GB = 16
RW_HEAD = 64
RW_CHUNK = 64
GN_EPS = 64e-5
DEC_GROUP = 16
LANE = 128
VMEM_LIMIT = 48 * 1024 * 1024
MM_VMEM_BUDGET = 36 * 1024 * 1024

_NT = (((1,), (1,)), ((), ()))
_TN = (((0,), (0,)), ((), ()))


def _cparams(sem):
    return pltpu.CompilerParams(dimension_semantics=sem, vmem_limit_bytes=VMEM_LIMIT)


def _sigmoid(x):
    return 1.0 / (1.0 + jnp.exp(-x))


def _silu(x):
    return x * _sigmoid(x)


def _gelu_tanh(x):
    return 0.5 * x * (1.0 + jnp.tanh(0.7978845608028654 * (x + 0.044715 * x * x * x)))


def _softplus(x):
    return jnp.maximum(x, 0.0) + jnp.log(1.0 + jnp.exp(-jnp.abs(x)))


def _mm_body(*refs, pro, n_pro, epi, n_epi, n_out):
    pro_refs = refs[:n_pro]
    w_ref = refs[n_pro]
    epi_refs = refs[n_pro + 1:n_pro + 1 + n_epi]
    out_refs = refs[n_pro + 1 + n_epi:n_pro + 1 + n_epi + n_out]
    xs_ref = refs[-1]

    @pl.when(pl.program_id(1) == 0)
    def _():
        xs_ref[...] = pro(*[r[...] for r in pro_refs]).astype(BF16)

    acc = jnp.dot(xs_ref[...], w_ref[pl.program_id(1)], preferred_element_type=F32)
    epi(acc, epi_refs, out_refs)


def _mm(name, pro, pro_args, pro_specs, w, epi, epi_args, epi_specs, out_shapes, out_specs, tm, tn):
    m = pro_args[0].shape[0]
    k, n = w.shape
    assert m % tm == 0 and n % tn == 0, (name, m, tm, n, tn)
    body = functools.partial(_mm_body, pro=pro, n_pro=len(pro_args), epi=epi, n_epi=len(epi_args),
                             n_out=len(out_shapes))
    w = w.reshape(k, n // tn, tn).transpose(1, 0, 2)
    w_spec = pl.BlockSpec((n // tn, k, tn), lambda i, j: (0, 0, 0), pipeline_mode=pl.Buffered(1))
    return pl.pallas_call(
        body,
        grid=(m // tm, n // tn),
        in_specs=[*pro_specs, w_spec, *epi_specs],
        out_specs=out_specs,
        out_shape=out_shapes,
        scratch_shapes=[pltpu.VMEM((tm, k), BF16)],
        compiler_params=_cparams(("parallel", "arbitrary")),
        name=name,
    )(*pro_args, w, *epi_args)


def _row_spec(tm, k):
    return pl.BlockSpec((tm, k), lambda i, j: (i, 0))


def _vec_spec(k):
    return pl.BlockSpec((1, k), lambda i, j: (0, 0))


def _tile_spec(tm, tn):
    return pl.BlockSpec((tm, tn), lambda i, j: (i, j))


def _nvec_spec(tn):
    return pl.BlockSpec((1, tn), lambda i, j: (0, j))


def _pro_rms(x, g):
    return x * lax.rsqrt(jnp.mean(x * x, axis=-1, keepdims=True) + EPS) * g


def _pro_gate(a, b):
    return a * _silu(b)


def _pro_mix(h, prev, mu):
    return h + (prev - h) * mu


def _pro_cast(x):
    return x


def _pro_tanh(x):
    return jnp.tanh(x)


def _epi_plain(acc, epi_refs, out_refs):
    out_refs[0][...] = acc.astype(out_refs[0].dtype)


def _epi_res(acc, epi_refs, out_refs):
    out_refs[0][...] = epi_refs[0][...] + acc


def _epi_glu(acc, epi_refs, out_refs):
    g_ref, b_ref = epi_refs
    out_refs[0][...] = g_ref[...] * _sigmoid(acc + b_ref[...])


def _pick_tm(m):
    return 512 if m % 512 == 0 else m


def _pick_tn(n):
    for t in (2048, 1024, 512, 256, 128):
        if n % t == 0:
            return t
    return n


def _mm_vmem_bytes(tm, tn, k, n, n_row_in, n_tile_epi, out_itemsize):
    return (2 * 4 * tm * k * n_row_in + 2 * k * n + 2 * 4 * tm * tn * n_tile_epi + 2 * out_itemsize * tm * tn
            + 2 * tm * k + 2 * 4 * tm * tn)


def _mm_simple(name, pro, pro_args, pro_kinds, w, epi=_epi_plain, epi_args=(), epi_kinds=(),
               out_dtype=F32):
    m = pro_args[0].shape[0]
    k, n = w.shape
    tn = _pick_tn(n)
    tm = _pick_tm(m)
    while tm > 128 and m % (tm // 2) == 0 and _mm_vmem_bytes(
            tm, tn, k, n, pro_kinds.count("row"), epi_kinds.count("tile"), jnp.dtype(out_dtype).itemsize) > MM_VMEM_BUDGET:
        tm //= 2
    pro_specs = [_row_spec(tm, k) if kd == "row" else _vec_spec(k) for kd in pro_kinds]
    epi_specs = [_tile_spec(tm, tn) if kd == "tile" else _nvec_spec(tn) for kd in epi_kinds]
    return _mm(name, pro, list(pro_args), pro_specs, w, epi, list(epi_args), epi_specs,
               [jax.ShapeDtypeStruct((m, n), out_dtype)], [_tile_spec(tm, tn)], tm, tn)[0]


def _rms_body(x_ref, g_ref, o_ref):
    o_ref[...] = _pro_rms(x_ref[...], g_ref[...])


def _rms_rows(x, g):
    m, k = x.shape
    tm = _pick_tm(m)
    return pl.pallas_call(
        _rms_body,
        grid=(m // tm,),
        in_specs=[pl.BlockSpec((tm, k), lambda i: (i, 0)), pl.BlockSpec((1, k), lambda i: (0, 0))],
        out_specs=pl.BlockSpec((tm, k), lambda i: (i, 0)),
        out_shape=jax.ShapeDtypeStruct((m, k), F32),
        compiler_params=_cparams(("parallel",)),
        name="rms_rows",
    )(x, g)


def _epi_qnope(acc, epi_refs, out_refs, *, four_d):
    g_ref, = epi_refs
    o_ref, = out_refs
    for hh in range(acc.shape[1] // LANE):
        a = acc[:, hh * LANE:(hh + 1) * LANE]
        y = a * lax.rsqrt(jnp.mean(a * a, axis=-1, keepdims=True) + EPS) * g_ref[...] * SM_SCALE
        if four_d:
            o_ref[0, hh] = y.astype(o_ref.dtype)
        else:
            o_ref[:, hh * LANE:(hh + 1) * LANE] = y.astype(o_ref.dtype)


def _rope_pair_tile(x, g2, c4, s4):
    lane = lax.broadcasted_iota(jnp.int32, x.shape, 1)
    lo = lane < D_ROPE
    x2 = x * x
    ss_e = jnp.sum(jnp.where(lo, x2, 0.0), axis=-1, keepdims=True)
    ss_o = jnp.sum(jnp.where(lo, 0.0, x2), axis=-1, keepdims=True)
    inv = jnp.where(lo, lax.rsqrt(ss_e * (1.0 / D_ROPE) + EPS), lax.rsqrt(ss_o * (1.0 / D_ROPE) + EPS))
    xn = x * inv * g2
    first_half = (lane % D_ROPE) < (D_ROPE // 2)
    partner = jnp.where(first_half, pltpu.roll(xn, LANE - D_ROPE // 2, 1), pltpu.roll(xn, D_ROPE // 2, 1))
    return (xn * c4 + partner * s4) * SM_SCALE, lo


def _epi_qrope(acc, epi_refs, out_refs, *, four_d):
    g2_ref, c4_ref, s4_ref = epi_refs
    o_ref, = out_refs
    for p in range(acc.shape[1] // LANE):
        r, lo = _rope_pair_tile(acc[:, p * LANE:(p + 1) * LANE], g2_ref[...], c4_ref[...], s4_ref[...])
        if four_d:
            o_ref[0, 2 * p] = jnp.where(lo, r, 0.0).astype(o_ref.dtype)
            o_ref[0, 2 * p + 1] = jnp.where(lo, 0.0, r).astype(o_ref.dtype)
        else:
            o_ref[:, p * LANE:(p + 1) * LANE] = r.astype(o_ref.dtype)


def _epi_latent(acc, epi_refs, out_refs):
    gkv_ref, g2_ref, g2p_ref, c4_ref, s4_ref = epi_refs
    lat_ref, cn_ref, kr_ref = out_refs
    c = acc[:, :KV_LORA]
    cn = c * lax.rsqrt(jnp.mean(c * c, axis=-1, keepdims=True) + EPS) * gkv_ref[...]
    lat_ref[:, :KV_LORA] = cn
    cn_ref[...] = cn.astype(cn_ref.dtype)
    d = acc[:, KV_LORA:KV_LORA + LANE]
    dp = acc[:, KV_LORA + LANE:KV_LORA + 2 * LANE]
    inv = lax.rsqrt(jnp.mean(d * d, axis=-1, keepdims=True) + EPS)
    r = d * inv * g2_ref[...] * c4_ref[...] + dp * inv * g2p_ref[...] * s4_ref[...]
    lat_ref[:, KV_LORA:LATENT] = r[:, :D_ROPE]
    kr_ref[...] = r.astype(kr_ref.dtype)


def _epi_knope(acc, epi_refs, out_refs):
    g_ref, = epi_refs
    o_ref, = out_refs
    for hh in range(acc.shape[1] // LANE):
        a = acc[:, hh * LANE:(hh + 1) * LANE]
        y = a * lax.rsqrt(jnp.mean(a * a, axis=-1, keepdims=True) + EPS) * g_ref[...]
        o_ref[0, hh] = y.astype(o_ref.dtype)


def _epi_heads(acc, epi_refs, out_refs):
    o_ref, = out_refs
    for hh in range(acc.shape[1] // LANE):
        o_ref[0, hh] = acc[:, hh * LANE:(hh + 1) * LANE].astype(o_ref.dtype)


def _flash_body(qt_ref, kt_ref, qn_ref, qr_ref, kn_ref, kr_ref, v_ref, o_ref, m_ref, l_ref, acc_ref, *, hb):
    t = pl.program_id(2)
    qi = qt_ref[t]
    ki = kt_ref[t]
    heads = range(hb)

    @pl.when(ki == 0)
    def _():
        m_ref[...] = jnp.full(m_ref.shape, NEG, F32)
        l_ref[...] = jnp.zeros(l_ref.shape, F32)
        acc_ref[...] = jnp.zeros(acc_ref.shape, F32)

    def step(masked):
        k_rope = kr_ref[0]
        qs = [jnp.concatenate([qn_ref[0, h], qr_ref[0, h]], axis=-1) for h in heads]
        ks = [jnp.concatenate([kn_ref[0, h], k_rope], axis=-1) for h in heads]
        ss = [lax.dot_general(q, k, _NT, preferred_element_type=F32) for q, k in zip(qs, ks)]
        if masked:
            keep = (lax.broadcasted_iota(jnp.int32, ss[0].shape, 1) <= lax.broadcasted_iota(jnp.int32, ss[0].shape, 0))
            ss = [jnp.where(keep, s, NEG) for s in ss]
        reps = ss[0].shape[1] // LANE
        m_old = [m_ref[h] for h in heads]
        m_new = [jnp.maximum(mo, jnp.max(s, axis=-1, keepdims=True)) for mo, s in zip(m_old, ss)]
        alpha = [jnp.exp(mo - mn) for mo, mn in zip(m_old, m_new)]
        ps = [jnp.exp(s - jnp.concatenate([mn] * reps, axis=1)) for s, mn in zip(ss, m_new)]
        pv = [jnp.dot(p.astype(BF16), v_ref[0, h], preferred_element_type=F32) for h, p in zip(heads, ps)]
        for h in heads:
            l_ref[h] = alpha[h] * l_ref[h] + jnp.sum(ps[h], axis=-1, keepdims=True)
            acc_ref[h] = alpha[h] * acc_ref[h] + pv[h]
            m_ref[h] = m_new[h]

    @pl.when(ki < qi)
    def _():
        step(False)

    @pl.when(ki == qi)
    def _():
        step(True)
        o_ref[0] = jnp.concatenate([acc_ref[h] / l_ref[h] for h in heads], axis=-1)


def _flash_attention(qn, qr, kn, kr, v):
    b, h, s, _ = qn.shape
    tq = min(512, s)
    nq = s // tq
    hb = 4
    pairs = [(qi, ki) for qi in range(nq) for ki in range(qi + 1)]
    qtab = jnp.asarray([p[0] for p in pairs], jnp.int32)
    ktab = jnp.asarray([p[1] for p in pairs], jnp.int32)
    q_spec = pl.BlockSpec((1, hb, tq, LANE), lambda bi, hi, t, qt, kt: (bi, hi, qt[t], 0))
    k_spec = pl.BlockSpec((1, hb, tq, LANE), lambda bi, hi, t, qt, kt: (bi, hi, kt[t], 0))
    grid_spec = pltpu.PrefetchScalarGridSpec(
        num_scalar_prefetch=2,
        grid=(b, h // hb, len(pairs)),
        in_specs=[q_spec, q_spec, k_spec,
                  pl.BlockSpec((1, tq, LANE), lambda bi, hi, t, qt, kt: (bi, kt[t], 0)), k_spec],
        out_specs=pl.BlockSpec((1, tq, hb * LANE), lambda bi, hi, t, qt, kt: (bi, qt[t], hi)),
        scratch_shapes=[pltpu.VMEM((hb, tq, LANE), F32), pltpu.VMEM((hb, tq, LANE), F32),
                        pltpu.VMEM((hb, tq, LANE), F32)],
    )
    return pl.pallas_call(
        functools.partial(_flash_body, hb=hb),
        grid_spec=grid_spec,
        out_shape=jax.ShapeDtypeStruct((b, s, h * LANE), F32),
        compiler_params=_cparams(("parallel", "parallel", "arbitrary")),
        name="mla_prompt_attn",
    )(qtab, ktab, qn, qr, kn, kr, v)


def _dec_attn_body(pt_ref, *refs, pps, group, n_steps, t_dec):
    cache_refs = refs[:pps]
    wuk_ref, qp_ref, qr_ref, cnew_ref, krnew_ref, wuv_ref = refs[pps:pps + 6]
    o_ref = refs[pps + 6]
    wq_ref, m_ref, l_ref, acc_ref = refs[pps + 7:]
    del pt_ref
    i = pl.program_id(1)
    n_up = HEADS * D_NOPE
    rows = t_dec * HEADS

    @pl.when(i == 0)
    def _():
        wq_ref[0:n_up, :] = wuk_ref[...]
        wq_ref[n_up:n_up + rows, :] = qp_ref[0]
        m_ref[...] = jnp.full(m_ref.shape, NEG, F32)
        l_ref[...] = jnp.zeros(l_ref.shape, F32)
        acc_ref[...] = jnp.zeros(acc_ref.shape, F32)

    def up_project(ct_bf):
        return jnp.dot(wq_ref[...], ct_bf, preferred_element_type=F32)

    def process(res, ct_bf, krt_bf, mask):
        n = ct_bf.shape[1]
        kn = res[:n_up].reshape(HEADS, D_NOPE, n)
        inv = lax.rsqrt(jnp.sum(kn * kn, axis=1) * (1.0 / D_NOPE) + EPS)
        s = res[n_up:] * jnp.concatenate([inv] * t_dec, axis=0)
        s = s + jnp.dot(qr_ref[0], krt_bf, preferred_element_type=F32)
        if mask is not None:
            s = jnp.where(mask, s, NEG)
        m_old = m_ref[...]
        m_new = jnp.maximum(m_old, jnp.max(s, axis=-1, keepdims=True))
        alpha = jnp.exp(m_old - m_new)
        p = jnp.exp(s - jnp.concatenate([m_new] * (n // LANE), axis=1))
        l_ref[...] = alpha * l_ref[...] + jnp.sum(p, axis=-1, keepdims=True)
        pc = lax.dot_general(p.astype(BF16), ct_bf, _NT, preferred_element_type=F32)
        acc_ref[...] = jnp.concatenate([alpha] * (KV_LORA // LANE), axis=1) * acc_ref[...] + pc
        m_ref[...] = m_new

    parts = []
    for u in range(0, pps, group):
        lats = [cache_refs[u + g][0, 0] for g in range(group)]
        ct_g = jnp.concatenate([lt[:KV_LORA].astype(BF16) for lt in lats], axis=1)
        krt_g = jnp.concatenate([lt[KV_LORA:LATENT].astype(BF16) for lt in lats], axis=1)
        parts.append((up_project(ct_g), ct_g, krt_g))
    process(*[jnp.concatenate(x, axis=1) if len(x) > 1 else x[0] for x in zip(*parts)], None)

    @pl.when(i == n_steps - 1)
    def _():
        npad = cnew_ref.shape[2]
        key = lax.broadcasted_iota(jnp.int32, (rows, npad), 1)
        tq = lax.broadcasted_iota(jnp.int32, (rows, npad), 0) // HEADS
        process(up_project(cnew_ref[0]), cnew_ref[0], krnew_ref[0], key <= tq)
        o_lat = (acc_ref[...] / jnp.concatenate([l_ref[...]] * (KV_LORA // LANE), axis=1)).astype(BF16)
        full = jnp.dot(o_lat, wuv_ref[...], preferred_element_type=F32)
        rh = lax.broadcasted_iota(jnp.int32, full.shape, 0) % HEADS
        ch = lax.broadcasted_iota(jnp.int32, full.shape, 1) // LANE
        diag = jnp.where(rh == ch, full, 0.0).reshape(t_dec, HEADS, full.shape[1])
        o_ref[0] = jnp.sum(diag, axis=1)


def _dec_attention(cache, page_table, wuk_t, qp, qr, cnew, krnew, wuv, layer):
    bdec, n_pages = page_table.shape
    rows = qp.shape[1]
    t_dec = rows // HEADS
    pps = 16 if n_pages % 16 == 0 else 2
    group = min(pps, DEC_GROUP)
    n_steps = n_pages // pps
    n_up = HEADS * D_NOPE

    def cache_spec(u):
        return pl.BlockSpec((1, 1, LATENT, PAGE), lambda b, i, pt: (layer, pt[b, i * pps + u], 0, 0))

    body = functools.partial(_dec_attn_body, pps=pps, group=group, n_steps=n_steps, t_dec=t_dec)
    grid_spec = pltpu.PrefetchScalarGridSpec(
        num_scalar_prefetch=1,
        grid=(bdec, n_steps),
        in_specs=[
            *[cache_spec(u) for u in range(pps)],
            pl.BlockSpec((n_up, KV_LORA), lambda b, i, pt: (0, 0)),
            pl.BlockSpec((1, rows, KV_LORA), lambda b, i, pt: (b, 0, 0)),
            pl.BlockSpec((1, rows, D_ROPE), lambda b, i, pt: (b, 0, 0)),
            pl.BlockSpec((1, KV_LORA, PAGE), lambda b, i, pt: (b, 0, 0)),
            pl.BlockSpec((1, D_ROPE, PAGE), lambda b, i, pt: (b, 0, 0)),
            pl.BlockSpec((KV_LORA, n_up), lambda b, i, pt: (0, 0)),
        ],
        out_specs=pl.BlockSpec((1, t_dec, n_up), lambda b, i, pt: (b, 0, 0)),
        scratch_shapes=[
            pltpu.VMEM((n_up + rows, KV_LORA), BF16),
            pltpu.VMEM((rows, LANE), F32),
            pltpu.VMEM((rows, LANE), F32),
            pltpu.VMEM((rows, KV_LORA), F32),
        ],
    )
    return pl.pallas_call(
        body,
        grid_spec=grid_spec,
        out_shape=jax.ShapeDtypeStruct((bdec, t_dec, n_up), F32),
        compiler_params=_cparams(("parallel", "arbitrary")),
        name="mla_sample_attn",
    )(page_table, *([cache] * pps), wuk_t, qp, qr, cnew, krnew, wuv)


def _absorb_body(qn_ref, g_ref, wuk_ref, o_ref):
    q = (qn_ref[...] * g_ref[...]).astype(BF16)
    o_ref[0] = lax.dot_general(q, wuk_ref[...], _NT, preferred_element_type=F32).astype(o_ref.dtype)


def _absorb_q(qn, g_kn, wuk):
    m = qn.shape[0]
    return pl.pallas_call(
        _absorb_body,
        grid=(HEADS,),
        in_specs=[
            pl.BlockSpec((m, LANE), lambda h: (0, h)),
            pl.BlockSpec((1, LANE), lambda h: (0, 0)),
            pl.BlockSpec((KV_LORA, LANE), lambda h: (0, h)),
        ],
        out_specs=pl.BlockSpec((1, m, KV_LORA), lambda h: (h, 0, 0)),
        out_shape=jax.ShapeDtypeStruct((HEADS, m, KV_LORA), BF16),
        compiler_params=_cparams(("parallel",)),
        name="mla_absorb_q",
    )(qn, g_kn, wuk)


def _rope_tables(pos):
    inv = ROPE_THETA ** (-jnp.arange(0, D_ROPE, 2, dtype=F32) / D_ROPE)
    ang = pos.astype(F32)[:, None] * inv[None, :]
    c, s = jnp.cos(ang), jnp.sin(ang)
    return jnp.tile(c, (1, 4)), jnp.concatenate([-s, s, -s, s], axis=1)


def _mla_weights(w_in, g_qn, g_qr, g_kv, g_kr, g_kn, w_uk, w_uv, w_o):
    d = w_in.shape[0]
    q_dim = HEADS * D_QK
    wq = w_in[:, :q_dim].reshape(d, HEADS, D_QK)
    w_kr = w_in[:, q_dim + KV_LORA:q_dim + LATENT]
    w_kr_sw = jnp.concatenate([w_kr[:, D_ROPE // 2:], w_kr[:, :D_ROPE // 2]], axis=1)
    g_kr_sw = jnp.concatenate([g_kr[D_ROPE // 2:], g_kr[:D_ROPE // 2]])
    wuk2 = w_uk.reshape(KV_LORA, HEADS * D_NOPE)
    return dict(
        w_qn=wq[:, :, :D_NOPE].reshape(d, HEADS * D_NOPE).astype(BF16),
        w_qr=wq[:, :, D_NOPE:].reshape(d, HEADS * D_ROPE).astype(BF16),
        w_lat=jnp.concatenate([w_in[:, q_dim:q_dim + KV_LORA], w_kr, w_kr, w_kr_sw, w_kr_sw], axis=1).astype(BF16),
        w_gate=w_in[:, q_dim + LATENT:].astype(BF16),
        g_qn=g_qn.reshape(1, D_NOPE),
        g_qr2=jnp.tile(g_qr, 2).reshape(1, LANE),
        g_kv=g_kv.reshape(1, KV_LORA),
        g_kr2=jnp.tile(g_kr, 2).reshape(1, LANE),
        g_kr2p=jnp.tile(g_kr_sw, 2).reshape(1, LANE),
        g_kn=g_kn.reshape(1, D_NOPE),
        wuk=wuk2.astype(BF16),
        wuk_t=wuk2.T.astype(BF16),
        wuv=w_uv.reshape(KV_LORA, HEADS * D_NOPE).astype(BF16),
        w_o=w_o.astype(BF16),
    )


def _mla_project(x2, g_norm, wts, c4, s4, seq_4d):
    m, d = x2.shape
    tm = _pick_tm(m)
    four_d = seq_4d is not None
    n_tab = c4.shape[0] // tm
    tab_spec = pl.BlockSpec((tm, LANE), lambda i, j: (i % n_tab, 0))
    lane_vec = pl.BlockSpec((1, LANE), lambda i, j: (0, 0))
    pro_args, pro_specs = [x2, g_norm], [_row_spec(tm, d), _vec_spec(d)]

    def q_out(n_heads_blk, n_cols):
        if four_d:
            nsb = seq_4d // tm
            shape = jax.ShapeDtypeStruct((m // seq_4d, HEADS, seq_4d, LANE), BF16)
            spec = pl.BlockSpec((1, n_heads_blk, tm, LANE), lambda i, j: (i // nsb, j, i % nsb, 0))
        else:
            shape = jax.ShapeDtypeStruct((m, n_cols), F32)
            spec = _tile_spec(tm, n_cols)
        return [shape], [spec]

    shp, spc = q_out(HEADS, HEADS * D_NOPE)
    qn = _mm("mla_qnope", _pro_rms, pro_args, pro_specs, wts["w_qn"],
             functools.partial(_epi_qnope, four_d=four_d), [wts["g_qn"]], [lane_vec], shp, spc, tm,
             HEADS * D_NOPE)[0]
    shp, spc = q_out(HEADS, HEADS * D_ROPE)
    qr = _mm("mla_qrope", _pro_rms, pro_args, pro_specs, wts["w_qr"],
             functools.partial(_epi_qrope, four_d=four_d), [wts["g_qr2"], c4, s4],
             [lane_vec, tab_spec, tab_spec], shp, spc, tm, HEADS * D_ROPE)[0]
    n_lat = wts["w_lat"].shape[1]
    lat, cn, kr = _mm(
        "mla_latent", _pro_rms, pro_args, pro_specs, wts["w_lat"], _epi_latent,
        [wts["g_kv"], wts["g_kr2"], wts["g_kr2p"], c4, s4],
        [pl.BlockSpec((1, KV_LORA), lambda i, j: (0, 0)), lane_vec, lane_vec, tab_spec, tab_spec],
        [jax.ShapeDtypeStruct((m, LATENT), F32), jax.ShapeDtypeStruct((m, KV_LORA), BF16),
         jax.ShapeDtypeStruct((m, LANE), BF16)],
        [pl.BlockSpec((tm, LATENT), lambda i, j: (i, 0)), pl.BlockSpec((tm, KV_LORA), lambda i, j: (i, 0)),
         pl.BlockSpec((tm, LANE), lambda i, j: (i, 0))],
        tm, n_lat)
    gate = _mm_simple("mla_gate", _pro_rms, pro_args, ["row", "vec"], wts["w_gate"])
    return qn, qr, lat, cn, kr, gate


def _mla_prompt(x, g_norm, wts, c4, s4):
    b, s, d = x.shape
    x2 = x.reshape(b * s, d)
    tm = _pick_tm(b * s)
    nsb = s // tm
    qn, qr, lat, cn, kr, gate = _mla_project(x2, g_norm, wts, c4, s4, s)
    head_shape = [jax.ShapeDtypeStruct((b, HEADS, s, LANE), BF16)]
    head_spec = [pl.BlockSpec((1, HEADS, tm, LANE), lambda i, j: (i // nsb, j, i % nsb, 0))]
    kn = _mm("mla_knope", _pro_cast, [cn], [_row_spec(tm, KV_LORA)], wts["wuk"], _epi_knope, [wts["g_kn"]],
             [pl.BlockSpec((1, LANE), lambda i, j: (0, 0))], head_shape, head_spec, tm, HEADS * D_NOPE)[0]
    v = _mm("mla_value", _pro_cast, [cn], [_row_spec(tm, KV_LORA)], wts["wuv"], _epi_heads, [], [],
            head_shape, head_spec, tm, HEADS * D_NOPE)[0]
    o = _flash_attention(qn, qr, kn, kr.reshape(b, s, LANE), v).reshape(b * s, HEADS * D_NOPE)
    y = _mm_simple("mla_out", _pro_gate, [o, gate], ["row", "row"], wts["w_o"], _epi_res, [x2], ["tile"])
    return y.reshape(b, s, d), lat.reshape(b, s, LATENT)


def _mla_sample(x, g_norm, wts, c4, s4, cache, page_table, layer):
    b, t, d = x.shape
    m = b * t
    x2 = x.reshape(m, d)
    qn, qr, lat, cn, kr, gate = _mla_project(x2, g_norm, wts, c4, s4, None)
    qp = _absorb_q(qn, wts["g_kn"], wts["wuk"])
    qp = qp.reshape(HEADS, b, t, KV_LORA).transpose(1, 2, 0, 3).reshape(b, t * HEADS, KV_LORA)
    qr = qr.reshape(b, t * HEADS, D_ROPE).astype(BF16)
    cnew = jnp.pad(cn.reshape(b, t, KV_LORA).transpose(0, 2, 1), ((0, 0), (0, 0), (0, PAGE - t)))
    krnew = jnp.pad(kr[:, :D_ROPE].reshape(b, t, D_ROPE).transpose(0, 2, 1), ((0, 0), (0, 0), (0, PAGE - t)))
    o = _dec_attention(cache, page_table, wts["wuk_t"], qp, qr, cnew, krnew, wts["wuv"], layer)
    o = o.reshape(m, HEADS * D_NOPE)
    y = _mm_simple("mla_out_s", _pro_gate, [o, gate], ["row", "row"], wts["w_o"], _epi_res, [x2], ["tile"])
    return y.reshape(b, t, d), lat.reshape(b, t, LATENT)


def _s5_body(u_ref, wb_ref, wc_ref, a_ref, apow_ref, d_ref, x0_ref, g_ref, xl_ref, bu_ref, carry_ref,
             *, seq, rows_per_seq_tile):
    tc = u_ref.shape[0]
    ns = S5_GB * S5_STATE
    long_seq = seq >= tc
    u = u_ref[...]
    bu_ref[...] = jnp.dot(u.astype(BF16), wb_ref[0], preferred_element_type=F32)
    a_re, a_im = a_ref[0, 0:1, :], a_ref[0, 1:2, :]

    def cmul_add(xr, xi, pr, pi, sr, si):
        return xr + pr * sr - pi * si, xi + pr * si + pi * sr

    if long_seq:
        t_idx = pl.program_id(1) % (seq // tc)

        @pl.when(t_idx == 0)
        def _():
            carry_ref[...] = x0_ref[0]

        row = lax.broadcasted_iota(jnp.int32, (8, ns), 0)
        pw = [(jnp.where(row >= k, apow_ref[0, 0, k - 1:k, :], 0.0), jnp.where(row >= k, apow_ref[0, 1, k - 1:k, :], 0.0))
              for k in (1, 2, 4)]
        ap_re, ap_im = apow_ref[0, 0], apow_ref[0, 1]

        def tile_step(j, carry):
            cr, ci = carry
            r0 = pl.multiple_of(j * 8, 8)
            xr = bu_ref[pl.ds(r0, 8), 0:ns]
            xi = bu_ref[pl.ds(r0, 8), ns:2 * ns]
            for k, (pr, pi) in zip((1, 2, 4), pw):
                xr, xi = cmul_add(xr, xi, pr, pi, pltpu.roll(xr, k, 0), pltpu.roll(xi, k, 0))
            xr, xi = cmul_add(xr, xi, ap_re, ap_im, cr, ci)
            bu_ref[pl.ds(r0, 8), 0:ns] = xr
            bu_ref[pl.ds(r0, 8), ns:2 * ns] = xi
            return xr[7:8, :], xi[7:8, :]

        cr, ci = lax.fori_loop(0, tc // 8, tile_step, (carry_ref[0:1, :], carry_ref[1:2, :]))
        carry_ref[0:1, :] = cr
        carry_ref[1:2, :] = ci
        xl_ref[0, 0:1, :] = cr
        xl_ref[0, 1:2, :] = ci
    else:
        xr = bu_ref[:, 0:ns]
        xi = bu_ref[:, ns:2 * ns]
        pos = lax.broadcasted_iota(jnp.int32, (tc, ns), 0) % seq
        k = 1
        pr, pi = a_re, a_im
        while k < seq:
            keep = pos >= k
            sr = jnp.where(keep, pltpu.roll(xr, k, 0), 0.0)
            si = jnp.where(keep, pltpu.roll(xi, k, 0), 0.0)
            xr, xi = cmul_add(xr, xi, pr, pi, sr, si)
            pr, pi = pr * pr - pi * pi, 2.0 * pr * pi
            k *= 2
        ap_re = jnp.concatenate([apow_ref[0, 0, 0:seq, :]] * (tc // seq), axis=0)
        ap_im = jnp.concatenate([apow_ref[0, 1, 0:seq, :]] * (tc // seq), axis=0)
        xr, xi = cmul_add(xr, xi, ap_re, ap_im, x0_ref[:, 0:ns], x0_ref[:, ns:2 * ns])
        bu_ref[:, 0:ns] = xr
        bu_ref[:, ns:2 * ns] = xi
        xl_ref[...] = bu_ref[...]

    y = jnp.dot(bu_ref[...].astype(BF16), wc_ref[0], preferred_element_type=F32) + d_ref[...] * u
    g_ref[...] = _gelu_tanh(y)


def _s5_discretize(lam_re, lam_im, log_dt, b_re, b_im, c_re, c_im):
    g, n = lam_re.shape
    dt = jnp.exp(log_dt)[:, None]
    mag = jnp.exp(lam_re * dt)
    ab_re, ab_im = mag * jnp.cos(lam_im * dt), mag * jnp.sin(lam_im * dt)
    nr, ni = ab_re - 1.0, ab_im
    den = lam_re * lam_re + lam_im * lam_im
    fr, fi = (nr * lam_re + ni * lam_im) / den, (ni * lam_re - nr * lam_im) / den
    bb_re = fr[..., None] * b_re - fi[..., None] * b_im
    bb_im = fr[..., None] * b_im + fi[..., None] * b_re
    nblk = g // S5_GB
    eye = jnp.eye(S5_GB, dtype=F32)

    def expand_b(bb):
        bb = bb.reshape(nblk, S5_GB, n, S5_GROUP)
        return jnp.einsum("bgnj,gh->bgjhn", bb, eye).reshape(nblk, S5_GB * S5_GROUP, S5_GB * n)

    def expand_c(cc):
        cc = cc.reshape(nblk, S5_GB, S5_GROUP, n)
        return jnp.einsum("bgjn,gh->bgnhj", cc, eye).reshape(nblk, S5_GB * n, S5_GB * S5_GROUP)

    wb = jnp.concatenate([expand_b(bb_re), expand_b(bb_im)], axis=2).astype(BF16)
    wc = jnp.concatenate([expand_c(c_re), -expand_c(c_im)], axis=1).astype(BF16)
    kk = jnp.arange(1, 9, dtype=F32)[:, None, None]
    magk = jnp.exp(kk * lam_re * dt)
    pk_re, pk_im = magk * jnp.cos(kk * lam_im * dt), magk * jnp.sin(kk * lam_im * dt)
    apow = jnp.stack([pk_re, pk_im], axis=0).reshape(2, 8, nblk, S5_GB * n).transpose(2, 0, 1, 3)
    a1 = jnp.stack([ab_re, ab_im], axis=0).reshape(2, nblk, S5_GB * n).transpose(1, 0, 2)
    return wb, wc, a1, apow


def _s5_mix(u_src, col0, x0_re, x0_im, prm, seq):
    wb, wc, a1, apow, d_skip = prm
    m = u_src.shape[0]
    nblk = wb.shape[0]
    cb = S5_GB * S5_GROUP
    ns = S5_GB * S5_STATE
    width = nblk * cb
    nseq = m // seq
    tc = _pick_tm(m) if seq >= 8 else m
    long_seq = seq >= tc
    n_tiles = m // tc
    cblk0 = col0 // cb
    x0 = jnp.concatenate([x0_re.reshape(nseq, nblk, ns), x0_im.reshape(nseq, nblk, ns)], axis=2)
    if long_seq:
        tiles_per_seq = seq // tc
        x0_arr = x0.transpose(1, 0, 2).reshape(nblk * nseq, 2, ns)
        x0_spec = pl.BlockSpec((1, 2, ns), lambda gb, r: (gb * nseq + r // tiles_per_seq, 0, 0))
        xl_shape = jax.ShapeDtypeStruct((nblk * nseq, 2, ns), F32)
        xl_spec = pl.BlockSpec((1, 2, ns), lambda gb, r: (gb * nseq + r // tiles_per_seq, 0, 0))
        carry_shape = (2, ns)
    else:
        x0_rows = jnp.repeat(x0, seq, axis=0)
        x0_arr = x0_rows.reshape(m, nblk * 2 * ns)
        x0_spec = pl.BlockSpec((tc, 2 * ns), lambda gb, r: (r, gb))
        xl_shape = jax.ShapeDtypeStruct((m, nblk * 2 * ns), F32)
        xl_spec = pl.BlockSpec((tc, 2 * ns), lambda gb, r: (r, gb))
        carry_shape = (2, ns)
    body = functools.partial(_s5_body, seq=seq, rows_per_seq_tile=tc)
    g, xl = pl.pallas_call(
        body,
        grid=(nblk, n_tiles),
        in_specs=[
            pl.BlockSpec((tc, cb), lambda gb, r: (r, cblk0 + gb)),
            pl.BlockSpec((1, cb, 2 * ns), lambda gb, r: (gb, 0, 0)),
            pl.BlockSpec((1, 2 * ns, cb), lambda gb, r: (gb, 0, 0)),
            pl.BlockSpec((1, 2, ns), lambda gb, r: (gb, 0, 0)),
            pl.BlockSpec((1, 2, 8, ns), lambda gb, r: (gb, 0, 0, 0)),
            pl.BlockSpec((1, cb), lambda gb, r: (0, gb)),
            x0_spec,
        ],
        out_specs=[pl.BlockSpec((tc, cb), lambda gb, r: (r, gb)), xl_spec],
        out_shape=[jax.ShapeDtypeStruct((m, width), F32), xl_shape],
        scratch_shapes=[pltpu.VMEM((tc, 2 * ns), F32), pltpu.VMEM(carry_shape, F32)],
        compiler_params=_cparams(("parallel", "arbitrary")),
        name="s5_mix",
    )(u_src, wb, wc, a1, apow, d_skip, x0_arr)
    if long_seq:
        xl = xl.reshape(nblk, nseq, 2, ns).transpose(2, 1, 0, 3).reshape(2, nseq, nblk * ns)
    else:
        xl = xl.reshape(nseq, seq, nblk, 2, ns)[:, seq - 1].transpose(2, 0, 1, 3).reshape(2, nseq, nblk * ns)
    return g, xl[0], xl[1]


def _s5_layer(x, g_norm, x0_re, x0_im, w_in, mix_prm, w_glu, b_glu, w_o):
    b, t, d = x.shape
    m = b * t
    x2 = x.reshape(m, d)
    width = w_glu.shape[0]
    z = _mm_simple("s5_in", _pro_rms, [x2, g_norm], ["row", "vec"], w_in)
    g, xr, xi = _s5_mix(z, 0, x0_re, x0_im, mix_prm, t)
    g2 = _mm_simple("s5_glu", _pro_cast, [g], ["row"], w_glu, _epi_glu, [g, b_glu], ["tile", "nvec"])
    tm = 256 if m % 256 == 0 else m
    gate_spec = pl.BlockSpec((tm, width), lambda i, j: (i, 1))
    y = _mm("s5_out", _pro_gate, [g2, z], [_row_spec(tm, width), gate_spec], w_o, _epi_res, [x2],
            [_tile_spec(tm, d)], [jax.ShapeDtypeStruct((m, d), F32)], [_tile_spec(tm, d)], tm, d)[0]
    ng = width // S5_GROUP
    return y.reshape(b, t, d), xr.reshape(b, ng, S5_STATE), xi.reshape(b, ng, S5_STATE)


def _rwkv_body(r_ref, k_ref, v_ref, dw_ref, da_ref, w0_ref, a0_ref, kk_ref, ka_ref, rk_ref, gw_ref, gb_ref,
               s0_ref, y_ref, sout_ref, state_ref, *, t_valid, n_chunks, pairs):
    lc = r_ref.shape[1]
    l2 = 2 * lc
    ci = pl.program_id(2)
    prs = range(pairs)

    def tile(x, p):
        return x[:, p * LANE:(p + 1) * LANE]

    @pl.when(ci == 0)
    def _():
        zero = jnp.zeros((RW_HEAD, RW_HEAD), F32)
        for p in prs:
            top = jnp.concatenate([s0_ref[0, 2 * p], zero], axis=1)
            bot = jnp.concatenate([zero, s0_ref[0, 2 * p + 1]], axis=1)
            state_ref[p] = jnp.concatenate([top, bot], axis=0)

    lane = lax.broadcasted_iota(jnp.int32, (lc, LANE), 1)
    lo = lane < RW_HEAD
    row = lax.broadcasted_iota(jnp.int32, (l2, l2), 0)
    col = lax.broadcasted_iota(jnp.int32, (l2, l2), 1)
    same = (row // lc) == (col // lc)
    strict = same & ((col % lc) < (row % lc))
    incl = same & ((col % lc) <= (row % lc))
    eye = (col == row).astype(F32)
    r1 = lax.broadcasted_iota(jnp.int32, (lc, lc), 0)
    c1 = lax.broadcasted_iota(jnp.int32, (lc, lc), 1)
    tri_incl = (c1 <= r1).astype(F32)

    def seg_sum(x):
        se = jnp.sum(jnp.where(lo, x, 0.0), axis=-1, keepdims=True)
        so = jnp.sum(jnp.where(lo, 0.0, x), axis=-1, keepdims=True)
        return jnp.where(lo, se, so)

    def stack(x):
        return jnp.concatenate([jnp.where(lo, x, 0.0), jnp.where(lo, 0.0, x)], axis=0)

    r = r_ref[0]
    k = k_ref[0]
    v = v_ref[0]
    wl = -_softplus(-(w0_ref[...] + dw_ref[0])) - 0.5
    lw = -jnp.exp(wl)
    a_rate = _sigmoid(a0_ref[...] + da_ref[0])
    kkr = k * kk_ref[...]
    kf = k * (1.0 + (a_rate - 1.0) * ka_ref[...])
    rkf = r * kf * rk_ref[...]
    if t_valid < lc:
        valid = lax.broadcasted_iota(jnp.int32, lw.shape, 0) < t_valid
        lw = jnp.where(valid, lw, 0.0)
        kkr = jnp.where(valid, kkr, 0.0)
        kf = jnp.where(valid, kf, 0.0)
    cl = jnp.dot(tri_incl, lw, preferred_element_type=F32, precision=lax.Precision.HIGHEST)
    p_in = jnp.exp(cl)
    p_ex = jnp.exp(cl - lw)
    p_inv = jnp.exp(-cl)
    p_end = jnp.exp(cl[lc - 1:lc, :] - cl)
    rt_full = r * p_in

    kk = [tile(kkr, p) for p in prs]
    kk = [x / jnp.maximum(jnp.sqrt(seg_sum(x * x)), 1e-12) for x in kk]
    bv = [kk[p] * tile(a_rate, p) for p in prs]
    at_st = [stack(-kk[p] * tile(p_ex, p)).astype(BF16) for p in prs]
    rt_st = [stack(tile(rt_full, p)).astype(BF16) for p in prs]
    bt_st = [stack(bv[p] * tile(p_inv, p)).astype(BF16) for p in prs]
    kt_st = [stack(tile(kf, p) * tile(p_inv, p)).astype(BF16) for p in prs]
    v_st = [stack(tile(v, p)).astype(BF16) for p in prs]
    bp_st = [stack(bv[p] * tile(p_end, p)).astype(BF16) for p in prs]
    kp_st = [stack(tile(kf, p) * tile(p_end, p)).astype(BF16) for p in prs]

    def nt(a, b):
        return lax.dot_general(a, b, _NT, preferred_element_type=F32)

    def nn(a, b):
        return jnp.dot(a.astype(BF16), b.astype(BF16), preferred_element_type=F32)

    if l2 % LANE == 0:
        amat = [nt(jnp.concatenate([at_st[p], rt_st[p]], axis=0), jnp.concatenate([bt_st[p], kt_st[p]], axis=0))
                for p in prs]
        n_ab = [jnp.where(strict, m[:l2, :l2], 0.0) for m in amat]
        a_ak = [jnp.where(strict, m[:l2, l2:], 0.0) for m in amat]
        a_rb = [jnp.where(incl, m[l2:, :l2], 0.0) for m in amat]
        a_rk = [jnp.where(incl, m[l2:, l2:], 0.0) for m in amat]
    else:
        n_ab = [jnp.where(strict, nt(at_st[p], bt_st[p]), 0.0) for p in prs]
        a_ak = [jnp.where(strict, nt(at_st[p], kt_st[p]), 0.0) for p in prs]
        a_rb = [jnp.where(incl, nt(rt_st[p], bt_st[p]), 0.0) for p in prs]
        a_rk = [jnp.where(incl, nt(rt_st[p], kt_st[p]), 0.0) for p in prs]

    tinv = [eye + n for n in n_ab]
    npow = n_ab
    span = 2
    while span < min(lc, t_valid):
        npow = [nn(n, n) for n in npow]
        tinv = [t + nn(n, t) for n, t in zip(npow, tinv)]
        span *= 2

    s_old = [state_ref[p] for p in prs]
    s_bf = [x.astype(BF16) for x in s_old]
    rhs_u = [nt(at_st[p], s_bf[p]) + nn(a_ak[p], v_st[p]) for p in prs]
    u_bf = [nn(tinv[p], rhs_u[p]).astype(BF16) for p in prs]
    y_st = [nt(rt_st[p], s_bf[p]) + nn(a_rb[p], u_bf[p]) + nn(a_rk[p], v_st[p]) for p in prs]
    for p in prs:
        s_new = s_old[p] * tile(p_in, p)[lc - 1:lc, :]
        s_new = s_new + lax.dot_general(u_bf[p], bp_st[p], _TN, preferred_element_type=F32)
        s_new = s_new + lax.dot_general(v_st[p], kp_st[p], _TN, preferred_element_type=F32)
        state_ref[p] = s_new

    inv_n = 1.0 / RW_HEAD
    outs = []
    for p in prs:
        y = y_st[p][:lc] + y_st[p][lc:]
        yc = y - seg_sum(y) * inv_n
        var = seg_sum(yc * yc) * inv_n
        yn = yc * lax.rsqrt(var + GN_EPS) * tile(gw_ref[...], p) + tile(gb_ref[...], p)
        outs.append(yn + seg_sum(tile(rkf, p)) * tile(v, p))
    y_ref[0] = jnp.concatenate(outs, axis=-1)

    @pl.when(ci == n_chunks - 1)
    def _():
        for p in prs:
            sbd = state_ref[p]
            sout_ref[0, 2 * p] = sbd[:RW_HEAD, :RW_HEAD]
            sout_ref[0, 2 * p + 1] = sbd[RW_HEAD:, RW_HEAD:]


def _rwkv_core(r, k, v, dw, da, vecs, s0, t_valid):
    b, t, w = r.shape
    lc = min(RW_CHUNK, t)
    n_chunks = t // lc
    pairs = min(w // LANE, 16)
    wb = pairs * LANE
    nhb = w // wb
    hpb = 2 * pairs
    seq_spec = pl.BlockSpec((1, lc, wb), lambda bi, hi, ci: (bi, ci, hi))
    vec_spec = pl.BlockSpec((1, wb), lambda bi, hi, ci: (0, hi))
    st_spec = pl.BlockSpec((1, hpb, RW_HEAD, RW_HEAD), lambda bi, hi, ci: (bi, hi, 0, 0))
    body = functools.partial(_rwkv_body, t_valid=t_valid, n_chunks=n_chunks, pairs=pairs)
    return pl.pallas_call(
        body,
        grid=(b, nhb, n_chunks),
        in_specs=[seq_spec] * 5 + [vec_spec] * 7 + [st_spec],
        out_specs=[seq_spec, st_spec],
        out_shape=[jax.ShapeDtypeStruct((b, t, w), F32), jax.ShapeDtypeStruct(s0.shape, F32)],
        scratch_shapes=[pltpu.VMEM((pairs, LANE, LANE), F32)],
        compiler_params=_cparams(("parallel", "parallel", "arbitrary")),
        name="rwkv7_core",
    )(r, k, v, dw, da, *vecs, s0)


def _rwkv_layer(x, g_norm, shift0, s0, wts):
    b, t, d = x.shape
    m = b * t
    x2 = x.reshape(m, d)
    h = _rms_rows(x2, g_norm)
    h3 = h.reshape(b, t, d)
    prev = jnp.concatenate([shift0[:, None, :], h3[:, :-1]], axis=1).reshape(m, d)
    mu = wts["mu"]

    def mixed(name, n, w):
        return _mm_simple(name, _pro_mix, [h, prev, mu[n:n + 1]], ["row", "row", "vec"], w)

    r = mixed("rwkv_r", 0, wts["w_r"])
    k = mixed("rwkv_k", 2, wts["w_k"])
    v = mixed("rwkv_v", 3, wts["w_v"])
    gpre = mixed("rwkv_g", 5, wts["w_g"])
    dw = _mm_simple("rwkv_w2", _pro_tanh, [mixed("rwkv_w1", 1, wts["w1"])], ["row"], wts["w2"])
    da = _mm_simple("rwkv_a2", _pro_cast, [mixed("rwkv_a1", 4, wts["a1"])], ["row"], wts["a2"])
    tp = t if t % 8 == 0 else 8 * ((t + 7) // 8)

    def seq3(a):
        a = a.reshape(b, t, -1)
        return a if tp == t else jnp.pad(a, ((0, 0), (0, tp - t), (0, 0)))

    y, s_fin = _rwkv_core(seq3(r), seq3(k), seq3(v), seq3(dw), seq3(da), wts["vecs"], s0, t)
    y = y[:, :t].reshape(m, -1)
    out = _mm_simple("rwkv_out", _pro_gate, [y, gpre], ["row", "row"], wts["w_o"], _epi_res, [x2], ["tile"])
    return out.reshape(b, t, d), h3[:, -1], s_fin


def _rwkv_weights(mu, w_r, w_k, w_v, w_g, w_o, w0, w1, w2, a0, a1, a2, k_k, k_a, r_k, gn_w, gn_b):
    lora = w1.shape[1]
    lpad = LANE * ((lora + LANE - 1) // LANE)

    def pad_cols(w):
        return jnp.pad(w, ((0, 0), (0, lpad - lora))).astype(BF16)

    def pad_rows(w):
        return jnp.pad(w, ((0, lpad - lora), (0, 0))).astype(BF16)

    row = lambda a: a.reshape(1, -1)
    return dict(
        mu=mu, w_r=w_r.astype(BF16), w_k=w_k.astype(BF16), w_v=w_v.astype(BF16), w_g=w_g.astype(BF16),
        w_o=w_o.astype(BF16), w1=pad_cols(w1), w2=pad_rows(w2), a1=pad_cols(a1), a2=pad_rows(a2),
        vecs=[row(w0), row(a0), row(k_k), row(k_a), row(r_k), row(gn_w), row(gn_b)],
    )


def kernel(x_prompt, x_sample, cache_mla, page_table, state_s5_re, state_s5_im, state_rwkv, state_rwkv_shift, norm_g, mla_w_in, mla_g_qn, mla_g_qr, mla_g_kv, mla_g_kr, mla_g_kn, mla_w_uk, mla_w_uv, mla_w_o, s5_w_in, s5_lam_re, s5_lam_im, s5_log_dt, s5_b_re, s5_b_im, s5_c_re, s5_c_im, s5_d, s5_w_glu, s5_b_glu, s5_w_o, rwkv_mu, rwkv_w_r, rwkv_w_k, rwkv_w_v, rwkv_w_g, rwkv_w_o, rwkv_w0, rwkv_w1, rwkv_w2, rwkv_a0, rwkv_a1, rwkv_a2, rwkv_k_k, rwkv_k_a, rwkv_r_k, rwkv_gn_w, rwkv_gn_b):
    depth = norm_g.shape[0]
    nb, seq, d = x_prompt.shape
    db, t_dec, _ = x_sample.shape
    past = page_table.shape[1] * PAGE
    c4p, s4p = _rope_tables(jnp.arange(seq, dtype=jnp.int32))
    c4s, s4s = _rope_tables(past + jnp.arange(t_dec, dtype=jnp.int32))
    c4s, s4s = jnp.tile(c4s, (db, 1)), jnp.tile(s4s, (db, 1))
    cache_t = jnp.swapaxes(cache_mla, 2, 3)
    yp, ys = x_prompt, x_sample
    mla_p, mla_s = [], []
    s5r_p, s5i_p, s5r_s, s5i_s = [], [], [], []
    rw_p, rw_s, sh_p, sh_s = [], [], [], []
    for i in range(depth):
        kind, j = i % 3, i // 3
        g_norm = norm_g[i].reshape(1, d)
        if kind == 0:
            wts = _mla_weights(mla_w_in[j], mla_g_qn[j], mla_g_qr[j], mla_g_kv[j], mla_g_kr[j], mla_g_kn[j],
                               mla_w_uk[j], mla_w_uv[j], mla_w_o[j])
            yp, lat = _mla_prompt(yp, g_norm, wts, c4p, s4p)
            mla_p.append(lat)
            ys, lat = _mla_sample(ys, g_norm, wts, c4s, s4s, cache_t, page_table, j)
            mla_s.append(lat)
        elif kind == 1:
            wb, wc, a1, apow = _s5_discretize(s5_lam_re[j], s5_lam_im[j], s5_log_dt[j], s5_b_re[j], s5_b_im[j],
                                              s5_c_re[j], s5_c_im[j])
            prm = (wb, wc, a1, apow, s5_d[j].reshape(1, -1))
            args = (s5_w_in[j].astype(BF16), prm, s5_w_glu[j].astype(BF16), s5_b_glu[j].reshape(1, -1),
                    s5_w_o[j].astype(BF16))
            ng, ns = s5_lam_re.shape[1], s5_lam_re.shape[2]
            zero = jnp.zeros((nb, ng, ns), F32)
            yp, xr, xi = _s5_layer(yp, g_norm, zero, zero, *args)
            s5r_p.append(xr)
            s5i_p.append(xi)
            ys, xr, xi = _s5_layer(ys, g_norm, state_s5_re[j], state_s5_im[j], *args)
            s5r_s.append(xr)
            s5i_s.append(xi)
        else:
            wts = _rwkv_weights(rwkv_mu[j], rwkv_w_r[j], rwkv_w_k[j], rwkv_w_v[j], rwkv_w_g[j], rwkv_w_o[j],
                                rwkv_w0[j], rwkv_w1[j], rwkv_w2[j], rwkv_a0[j], rwkv_a1[j], rwkv_a2[j],
                                rwkv_k_k[j], rwkv_k_a[j], rwkv_r_k[j].reshape(-1), rwkv_gn_w[j], rwkv_gn_b[j])
            nh = rwkv_r_k.shape[1]
            sh0 = jnp.zeros((nb, d), F32)
            st0 = jnp.zeros((nb, nh, RW_HEAD, RW_HEAD), F32)
            yp, sh, st = _rwkv_layer(yp, g_norm, sh0, st0, wts)
            sh_p.append(sh)
            rw_p.append(st)
            ys, sh, st = _rwkv_layer(ys, g_norm, state_rwkv_shift[j], state_rwkv[j], wts)
            sh_s.append(sh)
            rw_s.append(st)
    return (yp, ys, jnp.stack(mla_p), jnp.stack(mla_s),
            jnp.stack(s5r_p), jnp.stack(s5i_p), jnp.stack(s5r_s), jnp.stack(s5i_s),
            jnp.stack(rw_p), jnp.stack(rw_s), jnp.stack(sh_p), jnp.stack(sh_s))
```
